```python
import jax, jax.numpy as jnp
from jax import lax
import numpy as np

D_MODEL = 4096
BATCH = 1
SEQ = 16384
DEPTH = 4

GRID_W = 64
CTX_LEN = 256
N_MIXERS = 2
N_NA_LAYERS = (DEPTH + N_MIXERS - 1) // N_MIXERS
N_RG_LAYERS = DEPTH // N_MIXERS
NA_HEADS = 32
NA_HEAD_DIM = D_MODEL // NA_HEADS
WIN_R = 8
WIN_C = 16
D_RNN = D_MODEL
RG_BLOCKS = 16
RG_BLOCK = D_RNN // RG_BLOCKS
CONV_W = 4
RG_C = 8.0
PEER_HEADS = 8
N_KEYS = 80
N_EXPERTS = N_KEYS * N_KEYS
D_KEY = 256
D_HALF = D_KEY // 2
PEER_TOPK = 16
PEER_CHUNK = 64
ALPHA = (2 * DEPTH) ** 0.25
BETA = (8 * DEPTH) ** -0.25
LN_EPS = 1e-6

kernel_name = 'hybrid_na_rglru_peer_dit'


def layer_norm(x, g, b):
    xf = x.astype(jnp.float32)
    mu = jnp.mean(xf, axis=-1, keepdims=True)
    var = jnp.mean(jnp.square(xf - mu), axis=-1, keepdims=True)
    y = (xf - mu) * lax.rsqrt(var + LN_EPS) * g.astype(jnp.float32) + b.astype(jnp.float32)
    return y.astype(x.dtype)


def modulation(cond, w_ada, b_ada):
    m = jax.nn.silu(cond) @ w_ada + b_ada
    return jnp.split(m, 6, axis=-1)


def na_mixer(h_lat, h_ctx, w_qkv, w_o, rpb, ctx_out):
    B, S, _ = h_lat.shape
    rows = S // GRID_W
    kr = min(WIN_R, rows)
    n_loc = kr * WIN_C
    scale = NA_HEAD_DIM ** -0.5
    qkv = (h_lat @ w_qkv).reshape(B, rows, GRID_W, 3, NA_HEADS, NA_HEAD_DIM)
    qg, kg, vg = qkv[:, :, :, 0], qkv[:, :, :, 1], qkv[:, :, :, 2]
    qkv_c = (h_ctx @ w_qkv).reshape(B, h_ctx.shape[1], 3, NA_HEADS, NA_HEAD_DIM)
    qc, kc, vc = qkv_c[:, :, 0], qkv_c[:, :, 1], qkv_c[:, :, 2]
    r_start = jnp.clip(jnp.arange(rows) - kr // 2, 0, rows - kr)
    c_start = jnp.clip(jnp.arange(GRID_W) - WIN_C // 2, 0, GRID_W - WIN_C)
    c_idx = c_start[:, None] + jnp.arange(WIN_C)[None, :]
    dc = c_idx - jnp.arange(GRID_W)[:, None] + (WIN_C - 1)

    def row_block(args):
        r, rs, q_row = args
        k_win = lax.dynamic_slice_in_dim(kg, rs, kr, axis=1)[:, :, c_idx]
        v_win = lax.dynamic_slice_in_dim(vg, rs, kr, axis=1)[:, :, c_idx]
        dr = rs + jnp.arange(kr) - r + (WIN_R - 1)
        bias = rpb[:, dr[:, None, None], dc[None, :, :]]
        bias = jnp.transpose(bias, (0, 2, 1, 3)).astype(jnp.float32)
        s_loc = jnp.einsum('bwhd,bawkhd->bhwak', q_row, k_win).astype(jnp.float32) * scale + bias
        s_ctx = jnp.einsum('bwhd,bnhd->bhwn', q_row, kc).astype(jnp.float32) * scale
        s = jnp.concatenate([s_loc.reshape(B, NA_HEADS, GRID_W, n_loc), s_ctx], axis=-1)
        p = jax.nn.softmax(s, axis=-1).astype(v_win.dtype)
        p_loc = p[..., :n_loc].reshape(B, NA_HEADS, GRID_W, kr, WIN_C)
        return (jnp.einsum('bhwak,bawkhd->bwhd', p_loc, v_win)
                + jnp.einsum('bhwn,bnhd->bwhd', p[..., n_loc:], vc))

    o = lax.map(row_block, (jnp.arange(rows), r_start, jnp.moveaxis(qg, 1, 0)))
    y_lat = jnp.moveaxis(o, 0, 1).reshape(B, S, D_MODEL) @ w_o
    y_ctx = None
    if ctx_out:
        s = jnp.einsum('bqhd,bkhd->bhqk', qc, kc).astype(jnp.float32) * scale
        p = jax.nn.softmax(s, axis=-1).astype(vc.dtype)
        oc = jnp.einsum('bhqk,bkhd->bqhd', p, vc).reshape(B, h_ctx.shape[1], D_MODEL)
        y_ctx = oc @ w_o
    return y_lat, y_ctx


def centred_dwconv(x, w, b):
    T = x.shape[1]
    left = CONV_W // 2
    xp = jnp.pad(x, ((0, 0), (left, CONV_W - 1 - left), (0, 0)))
    y = xp[:, 0:T] * w[0]
    for k in range(1, CONV_W):
        y = y + xp[:, k:k + T] * w[k]
    return y + b


def linear_recurrence(a, u, h0):
    u = u.at[:, :, 0].add(a[:, :, 0] * h0)

    def combine(left, right):
        a_l, u_l = left
        a_r, u_r = right
        return a_r * a_l, a_r * u_l + u_r

    _, h = lax.associative_scan(combine, (a, u), axis=2)
    return h


def rglru_seq(h, h0, w_in, conv_w, conv_b, w_r, b_r, w_i, b_i, lam):
    B, T, _ = h.shape
    proj = h @ w_in
    gate_br, rec = proj[..., :D_RNN], proj[..., D_RNN:]
    rec = centred_dwconv(rec, conv_w, conv_b)
    rb = rec.reshape(B, T, RG_BLOCKS, RG_BLOCK)
    r = jax.nn.sigmoid((jnp.einsum('btnk,enkj->ebtnj', rb, w_r).reshape(2, B, T, D_RNN)
                        + b_r[:, None, None]).astype(jnp.float32))
    i = jax.nn.sigmoid((jnp.einsum('btnk,enkj->ebtnj', rb, w_i).reshape(2, B, T, D_RNN)
                        + b_i[:, None, None]).astype(jnp.float32))
    log_a = -RG_C * r * jax.nn.softplus(-lam.astype(jnp.float32))[:, None, None]
    a = jnp.exp(log_a)
    u = jnp.sqrt(-jnp.expm1(2.0 * log_a)) * i * rec.astype(jnp.float32)[None]
    flip = lambda t: jnp.stack([t[0], jnp.flip(t[1], axis=1)])
    hs = linear_recurrence(flip(a), flip(u), h0)
    y = hs[0] + jnp.flip(hs[1], axis=1)
    return y, gate_br, hs[:, :, -1]


def rg_mixer(h_lat, h_ctx, w_in, conv_w, conv_b, w_r, b_r, w_i, b_i, lam, w_out, ctx_out):
    B = h_ctx.shape[0]
    h0 = jnp.zeros((2, B, D_RNN), jnp.float32)
    y_c, g_c, h_ctx_final = rglru_seq(h_ctx, h0, w_in, conv_w, conv_b, w_r, b_r, w_i, b_i, lam)
    y_l, g_l, _ = rglru_seq(h_lat, h_ctx_final, w_in, conv_w, conv_b, w_r, b_r, w_i, b_i, lam)
    y_lat = (y_l.astype(h_lat.dtype) * jax.nn.gelu(g_l)) @ w_out
    y_ctx = None
    if ctx_out:
        y_ctx = (y_c.astype(h_ctx.dtype) * jax.nn.gelu(g_c)) @ w_out
    return y_lat, y_ctx


def peer(h, w_q, sub_keys, u_tab, v_tab):
    B, T, D = h.shape
    q = (h @ w_q).reshape(B, T, PEER_HEADS, 2, D_HALF)
    s = jnp.einsum('bthpk,pnk->bthpn', q, sub_keys).astype(jnp.float32)
    sv, si = lax.top_k(s, PEER_TOPK)
    cand = (sv[..., 0, :, None] + sv[..., 1, None, :]).reshape(B, T, PEER_HEADS, PEER_TOPK * PEER_TOPK)
    cid = (si[..., 0, :, None] * N_KEYS + si[..., 1, None, :]).reshape(B, T, PEER_HEADS, PEER_TOPK * PEER_TOPK)
    top, pos = lax.top_k(cand, PEER_TOPK)
    eid = jnp.take_along_axis(cid, pos, axis=-1)
    g = jax.nn.softmax(top, axis=-1)
    nc = (B * T) // PEER_CHUNK
    xs = h.reshape(nc, PEER_CHUNK, D)
    es = eid.reshape(nc, PEER_CHUNK, PEER_HEADS * PEER_TOPK)
    gs = g.reshape(nc, PEER_CHUNK, PEER_HEADS * PEER_TOPK)

    def expert_chunk(args):
        xc, ec, gc = args
        act = jnp.einsum('td,tkd->tk', xc, u_tab[ec])
        w = (gc * jax.nn.gelu(act.astype(jnp.float32))).astype(xc.dtype)
        return jnp.einsum('tk,tkd->td', w, v_tab[ec])

    y = lax.map(expert_chunk, (xs, es, gs))
    return y.reshape(B, T, D)


def setup_inputs(seed: int = 0) -> dict:
    key = jax.random.key(seed)
    ks = iter(jax.random.split(key, 24))
    f32 = jnp.float32
    d = D_MODEL

    def nrm(shape, scale):
        return jax.random.normal(next(ks), shape, f32) * scale

    x = nrm((BATCH, SEQ, d), 1.0)
    c = nrm((BATCH, d), 1.0)
    ctx = nrm((BATCH, CTX_LEN, d), 1.0)
    c_ctx = nrm((d,), 1.0)
    w_ada = nrm((DEPTH, d, 6 * d), d ** -0.5)
    b_ada = nrm((DEPTH, 6 * d), 0.02)
    ln_g = 1.0 + nrm((DEPTH, 2, d), 0.02)
    ln_b = nrm((DEPTH, 2, d), 0.02)
    na_w_qkv = nrm((N_NA_LAYERS, d, 3 * d), d ** -0.5)
    na_w_o = nrm((N_NA_LAYERS, d, d), BETA * d ** -0.5)
    na_rpb = nrm((N_NA_LAYERS, NA_HEADS, 2 * WIN_R - 1, 2 * WIN_C - 1), 0.1)
    rg_w_in = nrm((N_RG_LAYERS, d, 2 * D_RNN), d ** -0.5)
    rg_conv_w = nrm((N_RG_LAYERS, CONV_W, D_RNN), CONV_W ** -0.5)
    rg_conv_b = nrm((N_RG_LAYERS, D_RNN), 0.02)
    rg_w_r = nrm((N_RG_LAYERS, 2, RG_BLOCKS, RG_BLOCK, RG_BLOCK), RG_BLOCK ** -0.5)
    rg_b_r = nrm((N_RG_LAYERS, 2, D_RNN), 0.02)
    rg_w_i = nrm((N_RG_LAYERS, 2, RG_BLOCKS, RG_BLOCK, RG_BLOCK), RG_BLOCK ** -0.5)
    rg_b_i = nrm((N_RG_LAYERS, 2, D_RNN), 0.02)
    a_c = jax.random.uniform(next(ks), (N_RG_LAYERS, 2, D_RNN), f32, minval=0.9, maxval=0.999)
    a_base = a_c ** (1.0 / RG_C)
    rg_lam = jnp.log(a_base) - jnp.log1p(-a_base)
    rg_w_out = nrm((N_RG_LAYERS, D_RNN, d), BETA * D_RNN ** -0.5)
    peer_w_q = nrm((DEPTH, d, PEER_HEADS * D_KEY), d ** -0.5)
    peer_sub_keys = nrm((DEPTH, 2, N_KEYS, D_HALF), D_HALF ** -0.5)
    peer_u = nrm((DEPTH, N_EXPERTS, d), d ** -0.5)
    peer_v = nrm((DEPTH, N_EXPERTS, d), BETA)
    return {'x': x, 'c': c, 'ctx': ctx, 'c_ctx': c_ctx, 'w_ada': w_ada, 'b_ada': b_ada,
            'ln_g': ln_g, 'ln_b': ln_b, 'na_w_qkv': na_w_qkv, 'na_w_o': na_w_o, 'na_rpb': na_rpb,
            'rg_w_in': rg_w_in, 'rg_conv_w': rg_conv_w, 'rg_conv_b': rg_conv_b,
            'rg_w_r': rg_w_r, 'rg_b_r': rg_b_r, 'rg_w_i': rg_w_i, 'rg_b_i': rg_b_i,
            'rg_lam': rg_lam, 'rg_w_out': rg_w_out, 'peer_w_q': peer_w_q,
            'peer_sub_keys': peer_sub_keys, 'peer_u': peer_u, 'peer_v': peer_v}


def reference(x, c, ctx, c_ctx, w_ada, b_ada, ln_g, ln_b, na_w_qkv, na_w_o, na_rpb,
              rg_w_in, rg_conv_w, rg_conv_b, rg_w_r, rg_b_r, rg_w_i, rg_b_i, rg_lam, rg_w_out,
              peer_w_q, peer_sub_keys, peer_u, peer_v):
    for l in range(DEPTH):
        ctx_out = l < DEPTH - 1
        sh1, sc1, g1, sh2, sc2, g2 = [m[:, None, :] for m in modulation(c, w_ada[l], b_ada[l])]
        csh1, csc1, cg1, csh2, csc2, cg2 = modulation(c_ctx, w_ada[l], b_ada[l])
        h_lat = x * (1 + sc1) + sh1
        h_ctx = ctx * (1 + csc1) + csh1
        j = l // N_MIXERS
        if l % N_MIXERS == 0:
            y_lat, y_ctx = na_mixer(h_lat, h_ctx, na_w_qkv[j], na_w_o[j], na_rpb[j], ctx_out)
        else:
            y_lat, y_ctx = rg_mixer(h_lat, h_ctx, rg_w_in[j], rg_conv_w[j], rg_conv_b[j],
                                    rg_w_r[j], rg_b_r[j], rg_w_i[j], rg_b_i[j], rg_lam[j],
                                    rg_w_out[j], ctx_out)
        x = layer_norm(ALPHA * x + g1 * y_lat, ln_g[l, 0], ln_b[l, 0])
        h_lat = x * (1 + sc2) + sh2
        y_ffn = peer(h_lat, peer_w_q[l], peer_sub_keys[l], peer_u[l], peer_v[l])
        x = layer_norm(ALPHA * x + g2 * y_ffn, ln_g[l, 1], ln_b[l, 1])
        if ctx_out:
            ctx = layer_norm(ALPHA * ctx + cg1 * y_ctx, ln_g[l, 0], ln_b[l, 0])
            h_c = ctx * (1 + csc2) + csh2
            y_cf = peer(h_c, peer_w_q[l], peer_sub_keys[l], peer_u[l], peer_v[l])
            ctx = layer_norm(ALPHA * ctx + cg2 * y_cf, ln_g[l, 1], ln_b[l, 1])
    return x
```

```python
import functools

import jax
import jax.numpy as jnp
from jax import lax
from jax.experimental import pallas as pl
from jax.experimental.pallas import tpu as pltpu

F32 = jnp.float32
BF16 = jnp.bfloat16

GRID_W = 64
WIN_R = 8
WIN_C = 16
HEAD_DIM = 128
RG_C = 8.0
CONV_W = 4
PEER_HEADS = 8
PEER_TOPK = 16
LN_EPS = 1e-6

V7X_VMEM_BYTES = 64 * 1024 * 1024
LANE = 128
SUBLANE = 8
MXU_DIM = 256

NEG = -1e30
GROUP_LANES = 2 * HEAD_DIM


def _cparams(sem, vmem_mb):
    assert vmem_mb * 1024 * 1024 < V7X_VMEM_BYTES
    return pltpu.CompilerParams(dimension_semantics=sem, vmem_limit_bytes=vmem_mb * 1024 * 1024)


def _tile(n, target, unit):
    t = min(target, n) // unit * unit
    while n % t:
        t -= unit
    return t


def _gelu_tanh(x):
    return 0.5 * x * (1.0 + jnp.tanh(0.7978845608028654 * (x + 0.044715 * (x * x * x))))


def _sigmoid(x):
    return 1.0 / (1.0 + jnp.exp(-x))


def _mod_kernel(c_ref, w_ref, b_ref, o_ref):
    c = c_ref[...]
    s = (c * _sigmoid(c)).astype(BF16)
    o_ref[0] = jnp.dot(s, w_ref[0].astype(BF16), preferred_element_type=F32) + b_ref[0]


def _modulation(cond8, w_ada, b_ada):
    depth, d, n = w_ada.shape
    tn = 512
    return pl.pallas_call(
        _mod_kernel,
        grid=(depth, n // tn),
        in_specs=[pl.BlockSpec((SUBLANE, d), lambda l, j: (0, 0)),
                  pl.BlockSpec((1, d, tn), lambda l, j: (l, 0, j)),
                  pl.BlockSpec((1, 1, tn), lambda l, j: (l, 0, j))],
        out_specs=pl.BlockSpec((1, SUBLANE, tn), lambda l, j: (l, 0, j)),
        out_shape=jax.ShapeDtypeStruct((depth, SUBLANE, n), F32),
        compiler_params=_cparams(("arbitrary", "arbitrary"), 40),
        name="modulation",
    )(cond8, w_ada, b_ada.reshape(depth, 1, n))


def _mod_spec(d, layer, chunk):
    return pl.BlockSpec((1, SUBLANE, d), lambda *_: (layer, 0, chunk))


def _modulate_kernel(x_ref, sc_ref, sh_ref, o_ref, *, row):
    sc = sc_ref[0, row:row + 1, :]
    sh = sh_ref[0, row:row + 1, :]
    o_ref[...] = (x_ref[...] * (1.0 + sc) + sh).astype(BF16)


def _modulate(x, mod, layer, row):
    t, d = x.shape
    tr = min(256, t)
    return pl.pallas_call(
        functools.partial(_modulate_kernel, row=row),
        grid=(t // tr,),
        in_specs=[pl.BlockSpec((tr, d), lambda i: (i, 0)),
                  _mod_spec(d, layer, 1), _mod_spec(d, layer, 0)],
        out_specs=pl.BlockSpec((tr, d), lambda i: (i, 0)),
        out_shape=jax.ShapeDtypeStruct((t, d), BF16),
        compiler_params=_cparams(("arbitrary",), 32),
        name="modulate",
    )(x, mod, mod)


def _ln_kernel(x_ref, y_ref, g_ref, lg_ref, lb_ref, *rest, row, alpha, with_h):
    if with_h:
        sc_ref, sh_ref, xo_ref, ho_ref = rest
    else:
        (xo_ref,) = rest
    z = alpha * x_ref[...] + g_ref[0, row:row + 1, :] * y_ref[...].astype(F32)
    mu = jnp.mean(z, axis=-1, keepdims=True)
    zc = z - mu
    var = jnp.mean(zc * zc, axis=-1, keepdims=True)
    xn = zc * lax.rsqrt(var + LN_EPS) * lg_ref[...] + lb_ref[...]
    xo_ref[...] = xn
    if with_h:
        ho_ref[...] = (xn * (1.0 + sc_ref[0, row:row + 1, :]) + sh_ref[0, row:row + 1, :]).astype(BF16)


def _residual_ln(x, y, mod, layer, gate_chunk, ln_g, ln_b, row, alpha, nxt):
    t, d = x.shape
    tr = min(256, t)
    with_h = nxt is not None
    in_specs = [pl.BlockSpec((tr, d), lambda i: (i, 0)),
                pl.BlockSpec((tr, d), lambda i: (i, 0)),
                _mod_spec(d, layer, gate_chunk),
                pl.BlockSpec((1, d), lambda i: (0, 0)),
                pl.BlockSpec((1, d), lambda i: (0, 0))]
    args = [x, y, mod, ln_g.reshape(1, d), ln_b.reshape(1, d)]
    out_specs = [pl.BlockSpec((tr, d), lambda i: (i, 0))]
    out_shape = [jax.ShapeDtypeStruct((t, d), F32)]
    if with_h:
        in_specs += [_mod_spec(d, nxt[0], nxt[1]), _mod_spec(d, nxt[0], nxt[2])]
        args += [mod, mod]
        out_specs.append(pl.BlockSpec((tr, d), lambda i: (i, 0)))
        out_shape.append(jax.ShapeDtypeStruct((t, d), BF16))
    outs = pl.pallas_call(
        functools.partial(_ln_kernel, row=row, alpha=alpha, with_h=with_h),
        grid=(t // tr,),
        in_specs=in_specs, out_specs=out_specs, out_shape=out_shape,
        compiler_params=_cparams(("arbitrary",), 48),
        name="residual_ln",
    )(*args)
    return (outs[0], outs[1]) if with_h else (outs[0], None)


def _mm_kernel(a_ref, w_ref, o_ref):
    o_ref[...] = jnp.dot(a_ref[...], w_ref[...], preferred_element_type=F32).astype(o_ref.dtype)


def _matmul(a, w, out_dtype):
    m, k = a.shape
    _, n = w.shape
    tm = _tile(m, 1024, SUBLANE)
    tn = _tile(n, 1024, LANE)
    return pl.pallas_call(
        _mm_kernel,
        grid=(m // tm, n // tn),
        in_specs=[pl.BlockSpec((tm, k), lambda i, j: (i, 0)),
                  pl.BlockSpec((k, tn), lambda i, j: (0, j))],
        out_specs=pl.BlockSpec((tm, tn), lambda i, j: (i, j)),
        out_shape=jax.ShapeDtypeStruct((m, n), out_dtype),
        compiler_params=_cparams(("arbitrary", "arbitrary"), 48),
        name="matmul",
    )(a, w)


def _bias_kernel(rpb_ref, o_ref):
    h = pl.program_id(0)
    n_dr = 2 * WIN_R - 1
    n_dc = 2 * WIN_C - 1
    w = lax.broadcasted_iota(jnp.int32, (GRID_W, GRID_W), 0)
    col = lax.broadcasted_iota(jnp.int32, (GRID_W, GRID_W), 1)
    dcm = col - w + (WIN_C - 1)
    cs = jnp.clip(w - WIN_C // 2, 0, GRID_W - WIN_C)
    valid = (col >= cs) & (col < cs + WIN_C)
    planes = []
    for dr in range(n_dr):
        acc = jnp.full((GRID_W, GRID_W), NEG, F32)
        for dc in range(n_dc):
            acc = jnp.where(dcm == dc, rpb_ref[(h * n_dr + dr) * n_dc + dc], acc)
        planes.append(jnp.where(valid, acc, NEG))
    for off in range(WIN_R):
        o_ref[0, off] = jnp.concatenate([planes[off + a] for a in range(WIN_R)], axis=1)


def _bias_table(rpb):
    heads = rpb.shape[0]
    return pl.pallas_call(
        _bias_kernel,
        grid=(heads,),
        in_specs=[pl.BlockSpec(memory_space=pltpu.SMEM)],
        out_specs=pl.BlockSpec((1, WIN_R, GRID_W, WIN_R * GRID_W), lambda h: (h, 0, 0, 0)),
        out_shape=jax.ShapeDtypeStruct((heads, WIN_R, GRID_W, WIN_R * GRID_W), F32),
        compiler_params=_cparams(("arbitrary",), 32),
        name="na_bias_table",
    )(rpb.reshape(-1))


def _na_kernel(q_ref, k_ref, v_ref, kc_ref, vc_ref, tb_ref, o_ref, *, rows_per_step, n_rows, scale):
    blk = pl.program_id(1)
    win = WIN_R * GRID_W
    nt = (((1,), (1,)), ((), ()))

    def row_body(rl, carry):
        r = blk * rows_per_step + rl
        rs = jnp.clip(r - WIN_R // 2, 0, n_rows - WIN_R)
        off = rs - r + (WIN_R - 1)
        kstart = pl.multiple_of(rs * GRID_W, GRID_W)
        qstart = pl.multiple_of(rl * GRID_W, GRID_W)
        for hl in range(GROUP_LANES // HEAD_DIM):
            lanes = slice(hl * HEAD_DIM, (hl + 1) * HEAD_DIM)
            q = q_ref[pl.ds(qstart, GRID_W), lanes]
            kw = k_ref[pl.ds(kstart, win), lanes]
            vw = v_ref[pl.ds(kstart, win), lanes]
            s_loc = lax.dot_general(q, kw, nt, preferred_element_type=F32) * scale + tb_ref[hl, off]
            s_ctx = lax.dot_general(q, kc_ref[:, lanes], nt, preferred_element_type=F32) * scale
            m = jnp.maximum(jnp.max(s_loc, axis=-1, keepdims=True),
                            jnp.max(s_ctx, axis=-1, keepdims=True))
            p_loc = jnp.exp(s_loc - m)
            p_ctx = jnp.exp(s_ctx - m)
            denom = jnp.sum(p_loc, axis=-1, keepdims=True) + jnp.sum(p_ctx, axis=-1, keepdims=True)
            o = (jnp.dot(p_loc.astype(BF16), vw, preferred_element_type=F32)
                 + jnp.dot(p_ctx.astype(BF16), vc_ref[:, lanes], preferred_element_type=F32))
            o_ref[pl.ds(qstart, GRID_W), lanes] = (o / denom).astype(BF16)
        return carry

    lax.fori_loop(0, rows_per_step, row_body, 0)


def _na_attention(qkv, qkv_ctx, bias_tab):
    t, d3 = qkv.shape
    d = d3 // 3
    n_ctx = qkv_ctx.shape[0]
    n_rows = t // GRID_W
    rows_per_step = 8
    groups = d // GROUP_LANES
    tq = rows_per_step * GRID_W
    hg = GROUP_LANES // HEAD_DIM
    kern = functools.partial(_na_kernel, rows_per_step=rows_per_step, n_rows=n_rows,
                             scale=HEAD_DIM ** -0.5)
    return pl.pallas_call(
        kern,
        grid=(groups, n_rows // rows_per_step),
        in_specs=[pl.BlockSpec((tq, GROUP_LANES), lambda g, b: (b, g)),
                  pl.BlockSpec((t, GROUP_LANES), lambda g, b: (0, groups + g)),
                  pl.BlockSpec((t, GROUP_LANES), lambda g, b: (0, 2 * groups + g)),
                  pl.BlockSpec((n_ctx, GROUP_LANES), lambda g, b: (0, groups + g)),
                  pl.BlockSpec((n_ctx, GROUP_LANES), lambda g, b: (0, 2 * groups + g)),
                  pl.BlockSpec((hg, WIN_R, GRID_W, WIN_R * GRID_W), lambda g, b: (g, 0, 0, 0))],
        out_specs=pl.BlockSpec((tq, GROUP_LANES), lambda g, b: (b, g)),
        out_shape=jax.ShapeDtypeStruct((t, d), BF16),
        compiler_params=_cparams(("arbitrary", "arbitrary"), 48),
        name="na_attention",
    )(qkv, qkv, qkv, qkv_ctx, qkv_ctx, bias_tab)


def _ctx_attn_kernel(q_ref, k_ref, v_ref, o_ref, *, scale):
    nt = (((1,), (1,)), ((), ()))
    for hl in range(GROUP_LANES // HEAD_DIM):
        lanes = slice(hl * HEAD_DIM, (hl + 1) * HEAD_DIM)
        s = lax.dot_general(q_ref[:, lanes], k_ref[:, lanes], nt, preferred_element_type=F32) * scale
        m = jnp.max(s, axis=-1, keepdims=True)
        p = jnp.exp(s - m)
        denom = jnp.sum(p, axis=-1, keepdims=True)
        o = jnp.dot(p.astype(BF16), v_ref[:, lanes], preferred_element_type=F32)
        o_ref[:, lanes] = (o / denom).astype(BF16)


def _ctx_attention(qkv_ctx):
    n_ctx, d3 = qkv_ctx.shape
    d = d3 // 3
    groups = d // GROUP_LANES
    return pl.pallas_call(
        functools.partial(_ctx_attn_kernel, scale=HEAD_DIM ** -0.5),
        grid=(groups,),
        in_specs=[pl.BlockSpec((n_ctx, GROUP_LANES), lambda g: (0, g)),
                  pl.BlockSpec((n_ctx, GROUP_LANES), lambda g: (0, groups + g)),
                  pl.BlockSpec((n_ctx, GROUP_LANES), lambda g: (0, 2 * groups + g))],
        out_specs=pl.BlockSpec((n_ctx, GROUP_LANES), lambda g: (0, g)),
        out_shape=jax.ShapeDtypeStruct((n_ctx, d), BF16),
        compiler_params=_cparams(("arbitrary",), 32),
        name="ctx_attention",
    )(qkv_ctx, qkv_ctx, qkv_ctx)


def _scan_block(a, u, h_in, reverse):
    tb, c = a.shape
    ng = tb // SUBLANE
    a3 = a.reshape(ng, SUBLANE, c)
    u3 = u.reshape(ng, SUBLANE, c)
    j = lax.broadcasted_iota(jnp.int32, (ng, SUBLANE, c), 1)
    d = 1
    while d < SUBLANE:
        if reverse:
            keep = j < SUBLANE - d
            shift = SUBLANE - d
        else:
            keep = j >= d
            shift = d
        a_s = jnp.where(keep, pltpu.roll(a3, shift, 1), 1.0)
        u_s = jnp.where(keep, pltpu.roll(u3, shift, 1), 0.0)
        u3 = u3 + a3 * u_s
        a3 = a3 * a_s
        d *= 2
    hs = [None] * ng
    h = h_in
    order = range(ng - 1, -1, -1) if reverse else range(ng)
    edge = 0 if reverse else SUBLANE - 1
    for g in order:
        hg = u3[g] + a3[g] * h
        hs[g] = hg
        h = hg[edge:edge + 1, :]
    return jnp.concatenate(hs, axis=0), h


def _rg_kernel(curf_ref, prevf_ref, nextf_ref, curb_ref, prevb_ref, nextb_ref,
               cw_ref, cb_ref, wg_ref, br_ref, bi_ref, lam_ref, h0_ref,
               hsf_ref, hsb_ref, hfin_ref, xbuf_ref, carry_ref, *, tb, n_tb):
    i = pl.program_id(1)

    @pl.when(i == 0)
    def _():
        carry_ref[...] = h0_ref[...]

    cw = cw_ref[...]
    cb = cb_ref[...]
    halo = SUBLANE

    def gates(cur_ref, prev_ref, next_ref, blk, e):
        prev = jnp.where(blk == 0, 0.0, prev_ref[...])
        nxt = jnp.where(blk == n_tb - 1, 0.0, next_ref[...])
        xbuf_ref[e, 0:halo, :] = prev
        xbuf_ref[e, halo:halo + tb, :] = cur_ref[...]
        xbuf_ref[e, halo + tb:2 * halo + tb, :] = nxt
        left = CONV_W // 2
        xc = cb
        for k in range(CONV_W):
            xc = xc + xbuf_ref[e, halo - left + k:halo - left + k + tb, :] * cw[k:k + 1, :]
        c = xc.shape[1]
        z = jnp.dot(xc.astype(BF16), wg_ref[0, :, 2 * c * e:2 * c * (e + 1)],
                    preferred_element_type=F32)
        r = _sigmoid(z[:, :c] + br_ref[e:e + 1, :])
        ig = _sigmoid(z[:, c:] + bi_ref[e:e + 1, :])
        lam = lam_ref[e:e + 1, :]
        softplus = jnp.maximum(-lam, 0.0) + jnp.log(1.0 + jnp.exp(-jnp.abs(lam)))
        log_a = (-RG_C) * r * softplus
        a = jnp.exp(log_a)
        u = jnp.sqrt(1.0 - jnp.exp(2.0 * log_a)) * ig * xc
        return a, u

    a_f, u_f = gates(curf_ref, prevf_ref, nextf_ref, i, 0)
    h_f, last_f = _scan_block(a_f, u_f, carry_ref[0:1, :], reverse=False)
    hsf_ref[...] = h_f
    carry_ref[0:1, :] = last_f

    a_b, u_b = gates(curb_ref, prevb_ref, nextb_ref, n_tb - 1 - i, 1)
    h_b, last_b = _scan_block(a_b, u_b, carry_ref[1:2, :], reverse=True)
    hsb_ref[...] = h_b
    carry_ref[1:2, :] = last_b

    @pl.when(i == n_tb - 1)
    def _():
        hfin_ref[...] = carry_ref[...]


def _rg_scan(proj, conv_w, conv_b, w_gate, b_r, b_i, lam, h0):
    t, d2 = proj.shape
    d = d2 // 2
    c = MXU_DIM
    nb = d // c
    tb = min(256, t)
    n_tb = t // tb
    hb = tb // SUBLANE
    n_halo = t // SUBLANE

    def cur_f(n, i): return (i, nb + n)
    def prev_f(n, i): return (jnp.maximum(i * hb - 1, 0), nb + n)
    def next_f(n, i): return (jnp.minimum((i + 1) * hb, n_halo - 1), nb + n)
    def cur_b(n, i): return (n_tb - 1 - i, nb + n)
    def prev_b(n, i): return (jnp.maximum((n_tb - 1 - i) * hb - 1, 0), nb + n)
    def next_b(n, i): return (jnp.minimum((n_tb - i) * hb, n_halo - 1), nb + n)

    row = lambda n, i: (0, n)
    kern = functools.partial(_rg_kernel, tb=tb, n_tb=n_tb)
    return pl.pallas_call(
        kern,
        grid=(nb, n_tb),
        in_specs=[pl.BlockSpec((tb, c), cur_f), pl.BlockSpec((SUBLANE, c), prev_f),
                  pl.BlockSpec((SUBLANE, c), next_f),
                  pl.BlockSpec((tb, c), cur_b), pl.BlockSpec((SUBLANE, c), prev_b),
                  pl.BlockSpec((SUBLANE, c), next_b),
                  pl.BlockSpec((CONV_W, c), row), pl.BlockSpec((1, c), row),
                  pl.BlockSpec((1, c, 4 * c), lambda n, i: (n, 0, 0)),
                  pl.BlockSpec((2, c), row), pl.BlockSpec((2, c), row), pl.BlockSpec((2, c), row),
                  pl.BlockSpec((SUBLANE, c), row)],
        out_specs=[pl.BlockSpec((tb, c), lambda n, i: (i, n)),
                   pl.BlockSpec((tb, c), lambda n, i: (n_tb - 1 - i, n)),
                   pl.BlockSpec((SUBLANE, c), row)],
        out_shape=[jax.ShapeDtypeStruct((t, d), F32), jax.ShapeDtypeStruct((t, d), F32),
                   jax.ShapeDtypeStruct((SUBLANE, d), F32)],
        scratch_shapes=[pltpu.VMEM((2, tb + 2 * SUBLANE, c), F32), pltpu.VMEM((SUBLANE, c), F32)],
        compiler_params=_cparams(("arbitrary", "arbitrary"), 32),
        name="rg_scan",
    )(proj, proj, proj, proj, proj, proj, conv_w, conv_b.reshape(1, d), w_gate, b_r, b_i, lam, h0)


def _rg_combine_kernel(hf_ref, hb_ref, g_ref, o_ref):
    o_ref[...] = ((hf_ref[...] + hb_ref[...]) * _gelu_tanh(g_ref[...])).astype(BF16)


def _rg_combine(hs_f, hs_b, proj):
    t, d = hs_f.shape
    tr = min(256, t)
    spec = pl.BlockSpec((tr, d), lambda i: (i, 0))
    return pl.pallas_call(
        _rg_combine_kernel,
        grid=(t // tr,),
        in_specs=[spec, spec, spec],
        out_specs=spec,
        out_shape=jax.ShapeDtypeStruct((t, d), BF16),
        compiler_params=_cparams(("arbitrary",), 40),
        name="rg_combine",
    )(hs_f, hs_b, proj)


def _topk_desc(s, k):
    n, tm = s.shape
    rows = lax.broadcasted_iota(jnp.int32, (n, tm), 0)
    out = []
    for _ in range(k):
        m = jnp.max(s, axis=0, keepdims=True)
        first = jnp.min(jnp.where(s == m, rows, n), axis=0, keepdims=True)
        s = jnp.where(rows == first, -jnp.inf, s)
        out.append(m)
    return jnp.concatenate(out, axis=0)


def _route_kernel(q_ref, keys_ref, s_ref, e_ref, tau_ref):
    nt = (((1,), (1,)), ((), ()))
    half = keys_ref.shape[2]
    s1 = lax.dot_general(keys_ref[0].astype(BF16), q_ref[:, :half], nt, preferred_element_type=F32)
    s2 = lax.dot_general(keys_ref[1].astype(BF16), q_ref[:, half:], nt, preferred_element_type=F32)
    top1 = _topk_desc(s1, PEER_TOPK)
    top2 = _topk_desc(s2, PEER_TOPK)
    cand = jnp.concatenate([top1[a:a + 1, :] + top2 for a in range(PEER_TOPK)], axis=0)
    work = cand
    tau = None
    for _ in range(PEER_TOPK):
        tau = jnp.max(work, axis=0, keepdims=True)
        work = jnp.where(work == tau, -jnp.inf, work)
    m1 = top1[0:1, :]
    m2 = top2[0:1, :]
    z = jnp.sum(jnp.where(cand >= tau, jnp.exp(cand - (m1 + m2)), 0.0), axis=0, keepdims=True)
    s_ref[0, 0] = s1
    s_ref[0, 1] = s2
    e_ref[0, 0] = jnp.exp(s1 - m1)
    e_ref[0, 1] = jnp.exp(s2 - m2) / z
    tau_ref[0] = jnp.broadcast_to(tau, tau_ref.shape[1:])


def _peer_route(q, sub_keys):
    t = q.shape[0]
    _, n_keys, half = sub_keys.shape
    tm = min(256, t)
    return pl.pallas_call(
        _route_kernel,
        grid=(t // tm, PEER_HEADS),
        in_specs=[pl.BlockSpec((tm, 2 * half), lambda i, h: (i, h)),
                  pl.BlockSpec((2, n_keys, half), lambda i, h: (0, 0, 0))],
        out_specs=[pl.BlockSpec((1, 2, n_keys, tm), lambda i, h: (h, 0, 0, i)),
                   pl.BlockSpec((1, 2, n_keys, tm), lambda i, h: (h, 0, 0, i)),
                   pl.BlockSpec((1, SUBLANE, tm), lambda i, h: (h, 0, i))],
        out_shape=[jax.ShapeDtypeStruct((PEER_HEADS, 2, n_keys, t), F32),
                   jax.ShapeDtypeStruct((PEER_HEADS, 2, n_keys, t), F32),
                   jax.ShapeDtypeStruct((PEER_HEADS, SUBLANE, t), F32)],
        compiler_params=_cparams(("arbitrary", "arbitrary"), 32),
        name="peer_route",
    )(q, sub_keys)


def _expert_kernel(x_ref, u_ref, v_ref, s1_ref, e1_ref, s2_ref, e2_ref, tau_ref, o_ref, acc_ref,
                   *, n_keys, rows_per_chunk, lane_blk):
    c = pl.program_id(1)
    tm = x_ref.shape[0]
    nt = (((1,), (1,)), ((), ()))
    act = lax.dot_general(u_ref[...], x_ref[...], nt, preferred_element_type=F32)
    pieces = []
    for il in range(rows_per_chunk):
        lane_parts = []
        for lb in range(tm // lane_blk):
            lanes = slice(lb * lane_blk, (lb + 1) * lane_blk)
            g = jnp.zeros((n_keys, lane_blk), F32)
            for h in range(PEER_HEADS):
                ssum = s1_ref[h, 0, il:il + 1, lanes] + s2_ref[h, 0, :, lanes]
                gate = e1_ref[h, 0, il:il + 1, lanes] * e2_ref[h, 0, :, lanes]
                g = g + jnp.where(ssum >= tau_ref[h, 0:1, lanes], gate, 0.0)
            a = act[il * n_keys:(il + 1) * n_keys, lanes]
            lane_parts.append((g * _gelu_tanh(a)).astype(BF16))
        pieces.append(jnp.concatenate(lane_parts, axis=1))
    w_t = jnp.concatenate(pieces, axis=0)
    y = lax.dot_general(w_t, v_ref[...], (((0,), (0,)), ((), ())), preferred_element_type=F32)

    @pl.when(c == 0)
    def _():
        acc_ref[...] = y

    @pl.when(c != 0)
    def _():
        acc_ref[...] += y

    @pl.when(c == pl.num_programs(1) - 1)
    def _():
        o_ref[...] = acc_ref[...].astype(o_ref.dtype)


def _peer_experts(x, u_tab, v_tab, s, e, tau):
    t, d = x.shape
    n_exp = u_tab.shape[0]
    n_keys = s.shape[2]
    rows_per_chunk = SUBLANE
    ce = rows_per_chunk * n_keys
    tm = min(512, t)
    kern = functools.partial(_expert_kernel, n_keys=n_keys, rows_per_chunk=rows_per_chunk,
                             lane_blk=min(256, tm))
    return pl.pallas_call(
        kern,
        grid=(t // tm, n_exp // ce),
        in_specs=[pl.BlockSpec((tm, d), lambda i, c: (i, 0), pipeline_mode=pl.Buffered(1)),
                  pl.BlockSpec((ce, d), lambda i, c: (c, 0)),
                  pl.BlockSpec((ce, d), lambda i, c: (c, 0)),
                  pl.BlockSpec((PEER_HEADS, 1, rows_per_chunk, tm), lambda i, c: (0, 0, c, i)),
                  pl.BlockSpec((PEER_HEADS, 1, rows_per_chunk, tm), lambda i, c: (0, 0, c, i)),
                  pl.BlockSpec((PEER_HEADS, 1, n_keys, tm), lambda i, c: (0, 1, 0, i)),
                  pl.BlockSpec((PEER_HEADS, 1, n_keys, tm), lambda i, c: (0, 1, 0, i)),
                  pl.BlockSpec((PEER_HEADS, SUBLANE, tm), lambda i, c: (0, 0, i))],
        out_specs=pl.BlockSpec((tm, d), lambda i, c: (i, 0)),
        out_shape=jax.ShapeDtypeStruct((t, d), BF16),
        scratch_shapes=[pltpu.VMEM((tm, d), F32)],
        compiler_params=_cparams(("arbitrary", "arbitrary"), 56),
        name="peer_experts",
    )(x, u_tab, v_tab, s, e, s, e, tau)


def _peer(h, w_q, sub_keys, u_tab, v_tab):
    q = _matmul(h, w_q, BF16)
    s, e, tau = _peer_route(q, sub_keys)
    return _peer_experts(h, u_tab, v_tab, s, e, tau)


def kernel(x, c, ctx, c_ctx, w_ada, b_ada, ln_g, ln_b, na_w_qkv, na_w_o, na_rpb, rg_w_in, rg_conv_w, rg_conv_b, rg_w_r, rg_b_r, rg_w_i, rg_b_i, rg_lam, rg_w_out, peer_w_q, peer_sub_keys, peer_u, peer_v):
    assert x.shape[0] == 1 and ctx.shape[0] == 1
    depth = w_ada.shape[0]
    d = x.shape[2]
    alpha = (2 * depth) ** 0.25
    n_mixers = 2
    lat, con = 0, 1

    cond8 = jnp.zeros((SUBLANE, d), F32).at[lat].set(c[0]).at[con].set(c_ctx)
    mod = _modulation(cond8, w_ada, b_ada)

    xl = x[0]
    xc = ctx[0]
    hl = _modulate(xl, mod, 0, lat)
    hc = _modulate(xc, mod, 0, con)

    for l in range(depth):
        ctx_out = l < depth - 1
        j = l // n_mixers
        yc = None
        if l % n_mixers == 0:
            w_qkv = na_w_qkv[j].astype(BF16)
            w_o = na_w_o[j].astype(BF16)
            qkv_l = _matmul(hl, w_qkv, BF16)
            qkv_c = _matmul(hc, w_qkv, BF16)
            bias_tab = _bias_table(na_rpb[j])
            yl = _matmul(_na_attention(qkv_l, qkv_c, bias_tab), w_o, F32)
            if ctx_out:
                yc = _matmul(_ctx_attention(qkv_c), w_o, F32)
        else:
            w_in = rg_w_in[j].astype(BF16)
            w_out = rg_w_out[j].astype(BF16)
            nb = rg_w_r.shape[2]
            w_gate = jnp.concatenate([rg_w_r[j, 0], rg_w_i[j, 0], rg_w_r[j, 1], rg_w_i[j, 1]],
                                     axis=-1).astype(BF16)
            assert w_gate.shape[0] == nb
            rg_args = (rg_conv_w[j], rg_conv_b[j], w_gate, rg_b_r[j], rg_b_i[j], rg_lam[j])
            proj_c = _matmul(hc, w_in, F32)
            hs_fc, hs_bc, h_fin = _rg_scan(proj_c, *rg_args, jnp.zeros((SUBLANE, d), F32))
            proj_l = _matmul(hl, w_in, F32)
            hs_f, hs_b, _ = _rg_scan(proj_l, *rg_args, h_fin)
            yl = _matmul(_rg_combine(hs_f, hs_b, proj_l), w_out, F32)
            if ctx_out:
                yc = _matmul(_rg_combine(hs_fc, hs_bc, proj_c), w_out, F32)

        w_q = peer_w_q[l].astype(BF16)
        u_tab = peer_u[l].astype(BF16)
        v_tab = peer_v[l].astype(BF16)
        nxt = (l + 1, 1, 0) if ctx_out else None

        xl, hl = _residual_ln(xl, yl, mod, l, 2, ln_g[l, 0], ln_b[l, 0], lat, alpha, (l, 4, 3))
        y_ffn = _peer(hl, w_q, peer_sub_keys[l], u_tab, v_tab)
        xl, hl = _residual_ln(xl, y_ffn, mod, l, 5, ln_g[l, 1], ln_b[l, 1], lat, alpha, nxt)
        if ctx_out:
            xc, hc = _residual_ln(xc, yc, mod, l, 2, ln_g[l, 0], ln_b[l, 0], con, alpha, (l, 4, 3))
            y_cf = _peer(hc, w_q, peer_sub_keys[l], u_tab, v_tab)
            xc, hc = _residual_ln(xc, y_cf, mod, l, 5, ln_g[l, 1], ln_b[l, 1], con, alpha, nxt)
    return xl[None]
```

```python
import functools

import jax
import jax.numpy as jnp
from jax import lax
from jax.experimental import pallas as pl
from jax.experimental.pallas import tpu as pltpu

F32 = jnp.float32
BF16 = jnp.bfloat16

GRID_W = 64
WIN_R = 8
WIN_C = 16
HEAD_DIM = 128
RG_C = 8.0
CONV_W = 4
PEER_HEADS = 8
PEER_TOPK = 16
LN_EPS = 1e-6

V7X_VMEM_BYTES = 64 * 1024 * 1024
LANE = 128
SUBLANE = 8
MXU_DIM = 256

NEG = -1e30
GROUP_LANES = 2 * HEAD_DIM


def _cparams(sem, vmem_mb, flags=None):
    assert vmem_mb * 1024 * 1024 < V7X_VMEM_BYTES
    return pltpu.CompilerParams(dimension_semantics=sem, vmem_limit_bytes=vmem_mb * 1024 * 1024,
                                flags=flags)


def _tile(n, target, unit):
    t = min(target, n) // unit * unit
    while n % t:
        t -= unit
    return t


def _gelu_tanh(x):
    return 0.5 * x * (1.0 + jnp.tanh(0.7978845608028654 * (x + 0.044715 * (x * x * x))))


def _sigmoid(x):
    return 1.0 / (1.0 + jnp.exp(-x))


def _mod_kernel(c_ref, w_ref, b_ref, o_ref):
    c = c_ref[...]
    s = (c * _sigmoid(c)).astype(BF16)
    o_ref[0] = jnp.dot(s, w_ref[0].astype(BF16), preferred_element_type=F32) + b_ref[0]


def _modulation(cond8, w_ada, b_ada):
    depth, d, n = w_ada.shape
    tn = 512
    return pl.pallas_call(
        _mod_kernel,
        grid=(depth, n // tn),
        in_specs=[pl.BlockSpec((SUBLANE, d), lambda l, j: (0, 0)),
                  pl.BlockSpec((1, d, tn), lambda l, j: (l, 0, j)),
                  pl.BlockSpec((1, 1, tn), lambda l, j: (l, 0, j))],
        out_specs=pl.BlockSpec((1, SUBLANE, tn), lambda l, j: (l, 0, j)),
        out_shape=jax.ShapeDtypeStruct((depth, SUBLANE, n), F32),
        compiler_params=_cparams(("arbitrary", "arbitrary"), 40),
        name="modulation",
    )(cond8, w_ada, b_ada.reshape(depth, 1, n))


def _mod_spec(d, layer, chunk):
    return pl.BlockSpec((1, SUBLANE, d), lambda *_: (layer, 0, chunk))


def _modulate_kernel(x_ref, sc_ref, sh_ref, o_ref, *, row):
    sc = sc_ref[0, row:row + 1, :]
    sh = sh_ref[0, row:row + 1, :]
    o_ref[...] = (x_ref[...] * (1.0 + sc) + sh).astype(BF16)


def _modulate(x, mod, layer, row):
    t, d = x.shape
    tr = min(256, t)
    return pl.pallas_call(
        functools.partial(_modulate_kernel, row=row),
        grid=(t // tr,),
        in_specs=[pl.BlockSpec((tr, d), lambda i: (i, 0)),
                  _mod_spec(d, layer, 1), _mod_spec(d, layer, 0)],
        out_specs=pl.BlockSpec((tr, d), lambda i: (i, 0)),
        out_shape=jax.ShapeDtypeStruct((t, d), BF16),
        compiler_params=_cparams(("arbitrary",), 32),
        name="modulate",
    )(x, mod, mod)


def _ln_kernel(x_ref, y_ref, g_ref, lg_ref, lb_ref, *rest, row, alpha, with_h, with_ht):
    hto_ref = None
    if with_ht:
        sc_ref, sh_ref, xo_ref, ho_ref, hto_ref = rest
    elif with_h:
        sc_ref, sh_ref, xo_ref, ho_ref = rest
    else:
        (xo_ref,) = rest
    z = alpha * x_ref[...] + g_ref[0, row:row + 1, :] * y_ref[...].astype(F32)
    mu = jnp.mean(z, axis=-1, keepdims=True)
    zc = z - mu
    var = jnp.mean(zc * zc, axis=-1, keepdims=True)
    xn = zc * lax.rsqrt(var + LN_EPS) * lg_ref[...] + lb_ref[...]
    xo_ref[...] = xn
    if with_h:
        h = xn * (1.0 + sc_ref[0, row:row + 1, :]) + sh_ref[0, row:row + 1, :]
        ho_ref[...] = h.astype(BF16)
        if with_ht:
            hto_ref[...] = h.T.astype(BF16)


def _residual_ln(x, y, mod, layer, gate_chunk, ln_g, ln_b, row, alpha, nxt, with_ht=False):
    t, d = x.shape
    tr = min(256, t)
    with_h = nxt is not None
    in_specs = [pl.BlockSpec((tr, d), lambda i: (i, 0)),
                pl.BlockSpec((tr, d), lambda i: (i, 0)),
                _mod_spec(d, layer, gate_chunk),
                pl.BlockSpec((1, d), lambda i: (0, 0)),
                pl.BlockSpec((1, d), lambda i: (0, 0))]
    args = [x, y, mod, ln_g.reshape(1, d), ln_b.reshape(1, d)]
    out_specs = [pl.BlockSpec((tr, d), lambda i: (i, 0))]
    out_shape = [jax.ShapeDtypeStruct((t, d), F32)]
    if with_h:
        in_specs += [_mod_spec(d, nxt[0], nxt[1]), _mod_spec(d, nxt[0], nxt[2])]
        args += [mod, mod]
        out_specs.append(pl.BlockSpec((tr, d), lambda i: (i, 0)))
        out_shape.append(jax.ShapeDtypeStruct((t, d), BF16))
    if with_ht:
        out_specs.append(pl.BlockSpec((d, tr), lambda i: (0, i)))
        out_shape.append(jax.ShapeDtypeStruct((d, t), BF16))
    outs = pl.pallas_call(
        functools.partial(_ln_kernel, row=row, alpha=alpha, with_h=with_h, with_ht=with_ht),
        grid=(t // tr,),
        in_specs=in_specs, out_specs=out_specs, out_shape=out_shape,
        compiler_params=_cparams(("arbitrary",), 48),
        name="residual_ln",
    )(*args)
    outs = list(outs) + [None] * (3 - len(outs))
    return outs[0], outs[1], outs[2]


def _mm_kernel(a_ref, w_ref, o_ref):
    o_ref[...] = jnp.dot(a_ref[...], w_ref[...], preferred_element_type=F32).astype(o_ref.dtype)


def _matmul(a, w, out_dtype):
    m, k = a.shape
    _, n = w.shape
    tm = _tile(m, 1024, SUBLANE)
    tn = _tile(n, 1024, LANE)
    return pl.pallas_call(
        _mm_kernel,
        grid=(m // tm, n // tn),
        in_specs=[pl.BlockSpec((tm, k), lambda i, j: (i, 0)),
                  pl.BlockSpec((k, tn), lambda i, j: (0, j))],
        out_specs=pl.BlockSpec((tm, tn), lambda i, j: (i, j)),
        out_shape=jax.ShapeDtypeStruct((m, n), out_dtype),
        compiler_params=_cparams(("arbitrary", "arbitrary"), 48),
        name="matmul",
    )(a, w)


def _bias_kernel(rpb_ref, o_ref):
    h = pl.program_id(0)
    n_dr = 2 * WIN_R - 1
    n_dc = 2 * WIN_C - 1
    w = lax.broadcasted_iota(jnp.int32, (GRID_W, GRID_W), 0)
    col = lax.broadcasted_iota(jnp.int32, (GRID_W, GRID_W), 1)
    dcm = col - w + (WIN_C - 1)
    cs = jnp.clip(w - WIN_C // 2, 0, GRID_W - WIN_C)
    valid = (col >= cs) & (col < cs + WIN_C)
    planes = []
    for dr in range(n_dr):
        acc = jnp.full((GRID_W, GRID_W), NEG, F32)
        for dc in range(n_dc):
            acc = jnp.where(dcm == dc, rpb_ref[(h * n_dr + dr) * n_dc + dc], acc)
        planes.append(jnp.where(valid, acc, NEG))
    for off in range(WIN_R):
        o_ref[0, off] = jnp.concatenate([planes[off + a] for a in range(WIN_R)], axis=1)


def _bias_table(rpb):
    heads = rpb.shape[0]
    return pl.pallas_call(
        _bias_kernel,
        grid=(heads,),
        in_specs=[pl.BlockSpec(memory_space=pltpu.SMEM)],
        out_specs=pl.BlockSpec((1, WIN_R, GRID_W, WIN_R * GRID_W), lambda h: (h, 0, 0, 0)),
        out_shape=jax.ShapeDtypeStruct((heads, WIN_R, GRID_W, WIN_R * GRID_W), F32),
        compiler_params=_cparams(("arbitrary",), 32),
        name="na_bias_table",
    )(rpb.reshape(-1))


def _na_kernel(q_ref, k_ref, v_ref, kc_ref, vc_ref, tb_ref, o_ref, *, rows_per_step, n_rows, scale):
    blk = pl.program_id(1)
    win = WIN_R * GRID_W
    nt = (((1,), (1,)), ((), ()))
    for hl in range(GROUP_LANES // HEAD_DIM):
        lanes = slice(hl * HEAD_DIM, (hl + 1) * HEAD_DIM)
        q_all = q_ref[:, lanes]
        s_ctx = lax.dot_general(q_all, kc_ref[:, lanes], nt, preferred_element_type=F32) * scale
        m_ctx = jnp.max(s_ctx, axis=-1, keepdims=True)
        rows, kstarts, offs = [], [], []
        for rl in range(rows_per_step):
            r = blk * rows_per_step + rl
            rs = jnp.clip(r - WIN_R // 2, 0, n_rows - WIN_R)
            offs.append(rs - r + (WIN_R - 1))
            kstarts.append(pl.multiple_of(rs * GRID_W, GRID_W))
            rows.append(slice(rl * GRID_W, (rl + 1) * GRID_W))
        units = range(rows_per_step)
        s_locs = [lax.dot_general(q_all[rows[i]], k_ref[pl.ds(kstarts[i], win), lanes], nt,
                                  preferred_element_type=F32) * scale + tb_ref[hl, offs[i]]
                  for i in units]
        ms = [jnp.maximum(jnp.max(s_locs[i], axis=-1, keepdims=True), m_ctx[rows[i]])
              for i in units]
        ps = [jnp.exp(s_locs[i] - ms[i]) for i in units]
        ls = [jnp.sum(ps[i], axis=-1, keepdims=True) for i in units]
        os_ = [jnp.dot(ps[i].astype(BF16), v_ref[pl.ds(kstarts[i], win), lanes],
                       preferred_element_type=F32) for i in units]
        p_ctx = jnp.exp(s_ctx - jnp.concatenate(ms, axis=0))
        denom = jnp.concatenate(ls, axis=0) + jnp.sum(p_ctx, axis=-1, keepdims=True)
        o = jnp.concatenate(os_, axis=0) + jnp.dot(p_ctx.astype(BF16), vc_ref[:, lanes],
                                                   preferred_element_type=F32)
        o_ref[:, lanes] = (o / denom).astype(BF16)


def _na_attention(qkv, qkv_ctx, bias_tab):
    t, d3 = qkv.shape
    d = d3 // 3
    n_ctx = qkv_ctx.shape[0]
    n_rows = t // GRID_W
    rows_per_step = 8
    groups = d // GROUP_LANES
    tq = rows_per_step * GRID_W
    hg = GROUP_LANES // HEAD_DIM
    kern = functools.partial(_na_kernel, rows_per_step=rows_per_step, n_rows=n_rows,
                             scale=HEAD_DIM ** -0.5)
    return pl.pallas_call(
        kern,
        grid=(groups, n_rows // rows_per_step),
        in_specs=[pl.BlockSpec((tq, GROUP_LANES), lambda g, b: (b, g)),
                  pl.BlockSpec((t, GROUP_LANES), lambda g, b: (0, groups + g)),
                  pl.BlockSpec((t, GROUP_LANES), lambda g, b: (0, 2 * groups + g)),
                  pl.BlockSpec((n_ctx, GROUP_LANES), lambda g, b: (0, groups + g)),
                  pl.BlockSpec((n_ctx, GROUP_LANES), lambda g, b: (0, 2 * groups + g)),
                  pl.BlockSpec((hg, WIN_R, GRID_W, WIN_R * GRID_W), lambda g, b: (g, 0, 0, 0))],
        out_specs=pl.BlockSpec((tq, GROUP_LANES), lambda g, b: (b, g)),
        out_shape=jax.ShapeDtypeStruct((t, d), BF16),
        compiler_params=_cparams(("arbitrary", "arbitrary"), 48),
        name="na_attention",
    )(qkv, qkv, qkv, qkv_ctx, qkv_ctx, bias_tab)


def _ctx_attn_kernel(q_ref, k_ref, v_ref, o_ref, *, scale):
    nt = (((1,), (1,)), ((), ()))
    for hl in range(GROUP_LANES // HEAD_DIM):
        lanes = slice(hl * HEAD_DIM, (hl + 1) * HEAD_DIM)
        s = lax.dot_general(q_ref[:, lanes], k_ref[:, lanes], nt, preferred_element_type=F32) * scale
        m = jnp.max(s, axis=-1, keepdims=True)
        p = jnp.exp(s - m)
        denom = jnp.sum(p, axis=-1, keepdims=True)
        o = jnp.dot(p.astype(BF16), v_ref[:, lanes], preferred_element_type=F32)
        o_ref[:, lanes] = (o / denom).astype(BF16)


def _ctx_attention(qkv_ctx):
    n_ctx, d3 = qkv_ctx.shape
    d = d3 // 3
    groups = d // GROUP_LANES
    return pl.pallas_call(
        functools.partial(_ctx_attn_kernel, scale=HEAD_DIM ** -0.5),
        grid=(groups,),
        in_specs=[pl.BlockSpec((n_ctx, GROUP_LANES), lambda g: (0, g)),
                  pl.BlockSpec((n_ctx, GROUP_LANES), lambda g: (0, groups + g)),
                  pl.BlockSpec((n_ctx, GROUP_LANES), lambda g: (0, 2 * groups + g))],
        out_specs=pl.BlockSpec((n_ctx, GROUP_LANES), lambda g: (0, g)),
        out_shape=jax.ShapeDtypeStruct((n_ctx, d), BF16),
        compiler_params=_cparams(("arbitrary",), 32),
        name="ctx_attention",
    )(qkv_ctx, qkv_ctx, qkv_ctx)


def _scan_block(a, u, h_in, reverse):
    tb, c = a.shape
    ng = tb // SUBLANE
    a3 = a.reshape(ng, SUBLANE, c)
    u3 = u.reshape(ng, SUBLANE, c)
    j = lax.broadcasted_iota(jnp.int32, (ng, SUBLANE, c), 1)
    d = 1
    while d < SUBLANE:
        if reverse:
            keep = j < SUBLANE - d
            shift = SUBLANE - d
        else:
            keep = j >= d
            shift = d
        a_s = jnp.where(keep, pltpu.roll(a3, shift, 1), 1.0)
        u_s = jnp.where(keep, pltpu.roll(u3, shift, 1), 0.0)
        u3 = u3 + a3 * u_s
        a3 = a3 * a_s
        d *= 2
    hs = [None] * ng
    h = h_in
    order = range(ng - 1, -1, -1) if reverse else range(ng)
    edge = 0 if reverse else SUBLANE - 1
    for g in order:
        hg = u3[g] + a3[g] * h
        hs[g] = hg
        h = hg[edge:edge + 1, :]
    return jnp.concatenate(hs, axis=0), h


def _rg_kernel(curf_ref, prevf_ref, nextf_ref, curb_ref, prevb_ref, nextb_ref,
               cw_ref, cb_ref, wg_ref, br_ref, bi_ref, lam_ref, h0_ref,
               hsf_ref, hsb_ref, hfin_ref, xbuf_ref, carry_ref, *, tb, n_tb):
    i = pl.program_id(1)

    @pl.when(i == 0)
    def _():
        carry_ref[...] = h0_ref[...]

    cw = cw_ref[...]
    cb = cb_ref[...]
    halo = SUBLANE

    def gates(cur_ref, prev_ref, next_ref, blk, e):
        prev = jnp.where(blk == 0, 0.0, prev_ref[...])
        nxt = jnp.where(blk == n_tb - 1, 0.0, next_ref[...])
        xbuf_ref[e, 0:halo, :] = prev
        xbuf_ref[e, halo:halo + tb, :] = cur_ref[...]
        xbuf_ref[e, halo + tb:2 * halo + tb, :] = nxt
        left = CONV_W // 2
        xc = cb
        for k in range(CONV_W):
            xc = xc + xbuf_ref[e, halo - left + k:halo - left + k + tb, :] * cw[k:k + 1, :]
        c = xc.shape[1]
        z = jnp.dot(xc.astype(BF16), wg_ref[0, :, 2 * c * e:2 * c * (e + 1)],
                    preferred_element_type=F32)
        r = _sigmoid(z[:, :c] + br_ref[e:e + 1, :])
        ig = _sigmoid(z[:, c:] + bi_ref[e:e + 1, :])
        lam = lam_ref[e:e + 1, :]
        softplus = jnp.maximum(-lam, 0.0) + jnp.log(1.0 + jnp.exp(-jnp.abs(lam)))
        log_a = (-RG_C) * r * softplus
        a = jnp.exp(log_a)
        u = jnp.sqrt(1.0 - jnp.exp(2.0 * log_a)) * ig * xc
        return a, u

    a_f, u_f = gates(curf_ref, prevf_ref, nextf_ref, i, 0)
    h_f, last_f = _scan_block(a_f, u_f, carry_ref[0:1, :], reverse=False)
    hsf_ref[...] = h_f
    carry_ref[0:1, :] = last_f

    a_b, u_b = gates(curb_ref, prevb_ref, nextb_ref, n_tb - 1 - i, 1)
    h_b, last_b = _scan_block(a_b, u_b, carry_ref[1:2, :], reverse=True)
    hsb_ref[...] = h_b
    carry_ref[1:2, :] = last_b

    @pl.when(i == n_tb - 1)
    def _():
        hfin_ref[...] = carry_ref[...]


def _rg_scan(proj, conv_w, conv_b, w_gate, b_r, b_i, lam, h0):
    t, d2 = proj.shape
    d = d2 // 2
    c = MXU_DIM
    nb = d // c
    tb = min(512, t)
    n_tb = t // tb
    hb = tb // SUBLANE
    n_halo = t // SUBLANE

    def cur_f(n, i): return (i, nb + n)
    def prev_f(n, i): return (jnp.maximum(i * hb - 1, 0), nb + n)
    def next_f(n, i): return (jnp.minimum((i + 1) * hb, n_halo - 1), nb + n)
    def cur_b(n, i): return (n_tb - 1 - i, nb + n)
    def prev_b(n, i): return (jnp.maximum((n_tb - 1 - i) * hb - 1, 0), nb + n)
    def next_b(n, i): return (jnp.minimum((n_tb - i) * hb, n_halo - 1), nb + n)

    row = lambda n, i: (0, n)
    kern = functools.partial(_rg_kernel, tb=tb, n_tb=n_tb)
    return pl.pallas_call(
        kern,
        grid=(nb, n_tb),
        in_specs=[pl.BlockSpec((tb, c), cur_f), pl.BlockSpec((SUBLANE, c), prev_f),
                  pl.BlockSpec((SUBLANE, c), next_f),
                  pl.BlockSpec((tb, c), cur_b), pl.BlockSpec((SUBLANE, c), prev_b),
                  pl.BlockSpec((SUBLANE, c), next_b),
                  pl.BlockSpec((CONV_W, c), row), pl.BlockSpec((1, c), row),
                  pl.BlockSpec((1, c, 4 * c), lambda n, i: (n, 0, 0)),
                  pl.BlockSpec((2, c), row), pl.BlockSpec((2, c), row), pl.BlockSpec((2, c), row),
                  pl.BlockSpec((SUBLANE, c), row)],
        out_specs=[pl.BlockSpec((tb, c), lambda n, i: (i, n)),
                   pl.BlockSpec((tb, c), lambda n, i: (n_tb - 1 - i, n)),
                   pl.BlockSpec((SUBLANE, c), row)],
        out_shape=[jax.ShapeDtypeStruct((t, d), F32), jax.ShapeDtypeStruct((t, d), F32),
                   jax.ShapeDtypeStruct((SUBLANE, d), F32)],
        scratch_shapes=[pltpu.VMEM((2, tb + 2 * SUBLANE, c), F32), pltpu.VMEM((SUBLANE, c), F32)],
        compiler_params=_cparams(("arbitrary", "arbitrary"), 32),
        name="rg_scan",
    )(proj, proj, proj, proj, proj, proj, conv_w, conv_b.reshape(1, d), w_gate, b_r, b_i, lam, h0)


def _rg_combine_kernel(hf_ref, hb_ref, g_ref, o_ref):
    o_ref[...] = ((hf_ref[...] + hb_ref[...]) * _gelu_tanh(g_ref[...])).astype(BF16)


def _rg_combine(hs_f, hs_b, proj):
    t, d = hs_f.shape
    tr = min(256, t)
    spec = pl.BlockSpec((tr, d), lambda i: (i, 0))
    return pl.pallas_call(
        _rg_combine_kernel,
        grid=(t // tr,),
        in_specs=[spec, spec, spec],
        out_specs=spec,
        out_shape=jax.ShapeDtypeStruct((t, d), BF16),
        compiler_params=_cparams(("arbitrary",), 40),
        name="rg_combine",
    )(hs_f, hs_b, proj)


def _top_values(s, k):
    out = []
    for _ in range(k):
        m = jnp.max(s, axis=0, keepdims=True)
        s = jnp.where(s == m, -jnp.inf, s)
        out.append(m)
    return out


def _route_head(s1, s2):
    k = PEER_TOPK
    top1 = _top_values(s1, k)
    top2 = _top_values(s2, k)
    b16 = jnp.concatenate(top2, axis=0)
    b8 = b16[:SUBLANE]
    rank = lax.broadcasted_iota(jnp.int32, b8.shape, 0)
    pieces = [top1[0] + b16, top1[1] + b8]
    for a in range(2, SUBLANE):
        pieces.append(jnp.where(rank < k // (a + 1), top1[a] + b8, -jnp.inf))
    pieces.append(jnp.concatenate(top1[SUBLANE:], axis=0) + top2[0])
    cand = jnp.concatenate(pieces, axis=0)
    tau = _top_values(cand, k)[-1]
    m1, m2 = top1[0], top2[0]
    z = jnp.sum(jnp.where(cand >= tau, jnp.exp(cand - (m1 + m2)), 0.0), axis=0, keepdims=True)
    return tau, m1, m2, z


def _route_kernel(q_ref, keys_ref, s_ref, e_ref, tau_ref, *, heads_per_step):
    nt = (((1,), (1,)), ((), ()))
    half = keys_ref.shape[2]
    k1 = keys_ref[0].astype(BF16)
    k2 = keys_ref[1].astype(BF16)
    for h in range(heads_per_step):
        base = 2 * half * h
        s1 = lax.dot_general(k1, q_ref[:, base:base + half], nt, preferred_element_type=F32)
        s2 = lax.dot_general(k2, q_ref[:, base + half:base + 2 * half], nt,
                             preferred_element_type=F32)
        tau, m1, m2, z = _route_head(s1, s2)
        s_ref[h, 0] = s1
        s_ref[h, 1] = s2
        e_ref[h, 0] = jnp.exp(s1 - m1)
        e_ref[h, 1] = jnp.exp(s2 - m2) / z
        tau_ref[h] = jnp.broadcast_to(tau, tau_ref.shape[1:])


def _peer_route(q, sub_keys):
    t = q.shape[0]
    _, n_keys, half = sub_keys.shape
    tm = min(256, t)
    hps = 4
    return pl.pallas_call(
        functools.partial(_route_kernel, heads_per_step=hps),
        grid=(t // tm, PEER_HEADS // hps),
        in_specs=[pl.BlockSpec((tm, hps * 2 * half), lambda i, h: (i, h)),
                  pl.BlockSpec((2, n_keys, half), lambda i, h: (0, 0, 0))],
        out_specs=[pl.BlockSpec((hps, 2, n_keys, tm), lambda i, h: (h, 0, 0, i)),
                   pl.BlockSpec((hps, 2, n_keys, tm), lambda i, h: (h, 0, 0, i)),
                   pl.BlockSpec((hps, SUBLANE, tm), lambda i, h: (h, 0, i))],
        out_shape=[jax.ShapeDtypeStruct((PEER_HEADS, 2, n_keys, t), F32),
                   jax.ShapeDtypeStruct((PEER_HEADS, 2, n_keys, t), F32),
                   jax.ShapeDtypeStruct((PEER_HEADS, SUBLANE, t), F32)],
        compiler_params=_cparams(("arbitrary", "arbitrary"), 32),
        name="peer_route",
    )(q, sub_keys)


def _expert_kernel(x_ref, u_ref, v_ref, s1_ref, e1_ref, s2_ref, e2_ref, tau_ref, o_ref,
                   act_ref, g_ref, acc_ref, *, n_keys, rows_per_chunk, lane_blk, n_chunks):
    s = pl.program_id(0)
    slot = s % 2
    c = (s + n_chunks - 1) % n_chunks
    tm = x_ref.shape[1]

    @pl.when(s == 0)
    def _():
        act_ref[...] = jnp.zeros(act_ref.shape, F32)

    @pl.when((c == 0) | (s == 0))
    def _():
        acc_ref[...] = jnp.zeros(acc_ref.shape, F32)

    d = x_ref.shape[0]
    n_lb = tm // lane_blk
    k_pieces = rows_per_chunk
    kw = d // k_pieces
    for lb in range(n_lb):
        lanes = slice(lb * lane_blk, (lb + 1) * lane_blk)
        for il in range(rows_per_chunk):
            ks = slice(il * kw, (il + 1) * kw)
            part = jnp.dot(u_ref[:, ks], x_ref[ks, lanes], preferred_element_type=F32)
            if il == 0:
                act_ref[slot, :, lanes] = part
            else:
                act_ref[slot, :, lanes] += part

            for sub in range(lane_blk // LANE):
                ln = slice(lb * lane_blk + sub * LANE, lb * lane_blk + (sub + 1) * LANE)
                g = jnp.zeros((n_keys, LANE), F32)
                for h in range(PEER_HEADS):
                    ssum = s1_ref[h, 0, il:il + 1, ln] + s2_ref[h, 0, :, ln]
                    gate = e1_ref[h, 0, il:il + 1, ln] * e2_ref[h, 0, :, ln]
                    g = g + jnp.where(ssum >= tau_ref[h, 0:1, ln], gate, 0.0)
                g_ref[il * n_keys:(il + 1) * n_keys, ln] = g

    w_t = (g_ref[...] * _gelu_tanh(act_ref[1 - slot])).astype(BF16)
    acc_ref[...] += lax.dot_general(w_t, v_ref[...], (((0,), (0,)), ((), ())),
                                    preferred_element_type=F32)

    @pl.when(c == n_chunks - 1)
    def _():
        o_ref[...] = acc_ref[...].astype(o_ref.dtype)


def _peer_experts(x_t, u_tab, v_tab, s, e, tau):
    d, t = x_t.shape
    n_exp = u_tab.shape[0]
    n_keys = s.shape[2]
    rows_per_chunk = SUBLANE
    ce = rows_per_chunk * n_keys
    tm = min(512, t)
    n_chunks = n_exp // ce
    n_jobs = (t // tm) * n_chunks
    kern = functools.partial(_expert_kernel, n_keys=n_keys, rows_per_chunk=rows_per_chunk,
                             lane_blk=min(256, tm), n_chunks=n_chunks)

    def nxt(s): return jnp.minimum(s, n_jobs - 1)
    def cur(s): return jnp.maximum(s - 1, 0)
    once = pl.Buffered(1)
    return pl.pallas_call(
        kern,
        grid=(n_jobs + 1,),
        in_specs=[pl.BlockSpec((d, tm), lambda s: (0, nxt(s) // n_chunks), pipeline_mode=once),
                  pl.BlockSpec((ce, d), lambda s: (nxt(s) % n_chunks, 0)),
                  pl.BlockSpec((ce, d), lambda s: (cur(s) % n_chunks, 0)),
                  pl.BlockSpec((PEER_HEADS, 1, rows_per_chunk, tm),
                               lambda s: (0, 0, cur(s) % n_chunks, cur(s) // n_chunks)),
                  pl.BlockSpec((PEER_HEADS, 1, rows_per_chunk, tm),
                               lambda s: (0, 0, cur(s) % n_chunks, cur(s) // n_chunks)),
                  pl.BlockSpec((PEER_HEADS, 1, n_keys, tm),
                               lambda s: (0, 1, 0, cur(s) // n_chunks)),
                  pl.BlockSpec((PEER_HEADS, 1, n_keys, tm),
                               lambda s: (0, 1, 0, cur(s) // n_chunks)),
                  pl.BlockSpec((PEER_HEADS, SUBLANE, tm), lambda s: (0, 0, cur(s) // n_chunks))],
        out_specs=pl.BlockSpec((tm, d), lambda s: (cur(s) // n_chunks, 0)),
        out_shape=jax.ShapeDtypeStruct((t, d), BF16),
        scratch_shapes=[pltpu.VMEM((2, ce, tm), F32), pltpu.VMEM((ce, tm), F32),
                        pltpu.VMEM((tm, d), F32)],
        compiler_params=_cparams(("arbitrary",), 56),
        name="peer_experts",
    )(x_t, u_tab, v_tab, s, e, s, e, tau)


def _peer(h, h_t, w_q, sub_keys, u_tab, v_tab):
    q = _matmul(h, w_q, BF16)
    s, e, tau = _peer_route(q, sub_keys)
    return _peer_experts(h_t, u_tab, v_tab, s, e, tau)


def kernel(x, c, ctx, c_ctx, w_ada, b_ada, ln_g, ln_b, na_w_qkv, na_w_o, na_rpb, rg_w_in, rg_conv_w, rg_conv_b, rg_w_r, rg_b_r, rg_w_i, rg_b_i, rg_lam, rg_w_out, peer_w_q, peer_sub_keys, peer_u, peer_v):
    assert x.shape[0] == 1 and ctx.shape[0] == 1
    depth = w_ada.shape[0]
    d = x.shape[2]
    alpha = (2 * depth) ** 0.25
    n_mixers = 2
    lat, con = 0, 1

    cond8 = jnp.zeros((SUBLANE, d), F32).at[lat].set(c[0]).at[con].set(c_ctx)
    mod = _modulation(cond8, w_ada, b_ada)

    xl = x[0]
    xc = ctx[0]
    hl = _modulate(xl, mod, 0, lat)
    hc = _modulate(xc, mod, 0, con)

    for l in range(depth):
        ctx_out = l < depth - 1
        j = l // n_mixers
        yc = None
        if l % n_mixers == 0:
            w_qkv = na_w_qkv[j].astype(BF16)
            w_o = na_w_o[j].astype(BF16)
            qkv_l = _matmul(hl, w_qkv, BF16)
            qkv_c = _matmul(hc, w_qkv, BF16)
            bias_tab = _bias_table(na_rpb[j])
            yl = _matmul(_na_attention(qkv_l, qkv_c, bias_tab), w_o, F32)
            if ctx_out:
                yc = _matmul(_ctx_attention(qkv_c), w_o, F32)
        else:
            w_in = rg_w_in[j].astype(BF16)
            w_out = rg_w_out[j].astype(BF16)
            nb = rg_w_r.shape[2]
            w_gate = jnp.concatenate([rg_w_r[j, 0], rg_w_i[j, 0], rg_w_r[j, 1], rg_w_i[j, 1]],
                                     axis=-1).astype(BF16)
            assert w_gate.shape[0] == nb
            rg_args = (rg_conv_w[j], rg_conv_b[j], w_gate, rg_b_r[j], rg_b_i[j], rg_lam[j])
            proj_c = _matmul(hc, w_in, F32)
            hs_fc, hs_bc, h_fin = _rg_scan(proj_c, *rg_args, jnp.zeros((SUBLANE, d), F32))
            proj_l = _matmul(hl, w_in, F32)
            hs_f, hs_b, _ = _rg_scan(proj_l, *rg_args, h_fin)
            yl = _matmul(_rg_combine(hs_f, hs_b, proj_l), w_out, F32)
            if ctx_out:
                yc = _matmul(_rg_combine(hs_fc, hs_bc, proj_c), w_out, F32)

        w_q = peer_w_q[l].astype(BF16)
        u_tab = peer_u[l].astype(BF16)
        v_tab = peer_v[l].astype(BF16)
        nxt = (l + 1, 1, 0) if ctx_out else None

        xl, hl, hl_t = _residual_ln(xl, yl, mod, l, 2, ln_g[l, 0], ln_b[l, 0], lat, alpha,
                                    (l, 4, 3), with_ht=True)
        y_ffn = _peer(hl, hl_t, w_q, peer_sub_keys[l], u_tab, v_tab)
        xl, hl, _ = _residual_ln(xl, y_ffn, mod, l, 5, ln_g[l, 1], ln_b[l, 1], lat, alpha, nxt)
        if ctx_out:
            xc, hc, hc_t = _residual_ln(xc, yc, mod, l, 2, ln_g[l, 0], ln_b[l, 0], con, alpha,
                                        (l, 4, 3), with_ht=True)
            y_cf = _peer(hc, hc_t, w_q, peer_sub_keys[l], u_tab, v_tab)
            xc, hc, _ = _residual_ln(xc, y_cf, mod, l, 5, ln_g[l, 1], ln_b[l, 1], con, alpha, nxt)
    return xl[None]
```

```python
import functools

import jax
import jax.numpy as jnp
from jax import lax
from jax.experimental import pallas as pl
from jax.experimental.pallas import tpu as pltpu

F32 = jnp.float32
BF16 = jnp.bfloat16

GRID_W = 64
WIN_R = 8
WIN_C = 16
HEAD_DIM = 128
RG_C = 8.0
CONV_W = 4
PEER_HEADS = 8
PEER_TOPK = 16
LN_EPS = 1e-6

V7X_VMEM_BYTES = 64 * 1024 * 1024
LANE = 128
SUBLANE = 8
MXU_DIM = 256

NEG = -1e30
GROUP_LANES = 2 * HEAD_DIM


def _cparams(sem, vmem_mb, flags=None):
    assert vmem_mb * 1024 * 1024 < V7X_VMEM_BYTES
    return pltpu.CompilerParams(dimension_semantics=sem, vmem_limit_bytes=vmem_mb * 1024 * 1024,
                                flags=flags)


def _tile(n, target, unit):
    t = min(target, n) // unit * unit
    while n % t:
        t -= unit
    return t


def _gelu_tanh(x):
    return 0.5 * x * (1.0 + jnp.tanh(0.7978845608028654 * (x + 0.044715 * (x * x * x))))


def _sigmoid(x):
    return 1.0 / (1.0 + jnp.exp(-x))


def _mod_kernel(c_ref, w_ref, b_ref, o_ref):
    c = c_ref[...]
    s = (c * _sigmoid(c)).astype(BF16)
    o_ref[0] = jnp.dot(s, w_ref[0].astype(BF16), preferred_element_type=F32) + b_ref[0]


def _modulation(cond8, w_ada, b_ada):
    depth, d, n = w_ada.shape
    tn = 512
    return pl.pallas_call(
        _mod_kernel,
        grid=(depth, n // tn),
        in_specs=[pl.BlockSpec((SUBLANE, d), lambda l, j: (0, 0)),
                  pl.BlockSpec((1, d, tn), lambda l, j: (l, 0, j)),
                  pl.BlockSpec((1, 1, tn), lambda l, j: (l, 0, j))],
        out_specs=pl.BlockSpec((1, SUBLANE, tn), lambda l, j: (l, 0, j)),
        out_shape=jax.ShapeDtypeStruct((depth, SUBLANE, n), F32),
        compiler_params=_cparams(("arbitrary", "arbitrary"), 40),
        name="modulation",
    )(cond8, w_ada, b_ada.reshape(depth, 1, n))


def _mod_spec(d, layer, chunk):
    return pl.BlockSpec((1, SUBLANE, d), lambda *_: (layer, 0, chunk))


def _modulate_kernel(x_ref, sc_ref, sh_ref, o_ref, *, row):
    sc = sc_ref[0, row:row + 1, :]
    sh = sh_ref[0, row:row + 1, :]
    o_ref[...] = (x_ref[...] * (1.0 + sc) + sh).astype(BF16)


def _modulate(x, mod, layer, row):
    t, d = x.shape
    tr = min(256, t)
    return pl.pallas_call(
        functools.partial(_modulate_kernel, row=row),
        grid=(t // tr,),
        in_specs=[pl.BlockSpec((tr, d), lambda i: (i, 0)),
                  _mod_spec(d, layer, 1), _mod_spec(d, layer, 0)],
        out_specs=pl.BlockSpec((tr, d), lambda i: (i, 0)),
        out_shape=jax.ShapeDtypeStruct((t, d), BF16),
        compiler_params=_cparams(("arbitrary",), 32),
        name="modulate",
    )(x, mod, mod)


def _ln_kernel(x_ref, y_ref, g_ref, lg_ref, lb_ref, *rest, row, alpha, with_h, with_ht):
    hto_ref = None
    if with_ht:
        sc_ref, sh_ref, xo_ref, ho_ref, hto_ref = rest
    elif with_h:
        sc_ref, sh_ref, xo_ref, ho_ref = rest
    else:
        (xo_ref,) = rest
    z = alpha * x_ref[...] + g_ref[0, row:row + 1, :] * y_ref[...].astype(F32)
    mu = jnp.mean(z, axis=-1, keepdims=True)
    zc = z - mu
    var = jnp.mean(zc * zc, axis=-1, keepdims=True)
    xn = zc * lax.rsqrt(var + LN_EPS) * lg_ref[...] + lb_ref[...]
    xo_ref[...] = xn
    if with_h:
        h = xn * (1.0 + sc_ref[0, row:row + 1, :]) + sh_ref[0, row:row + 1, :]
        ho_ref[...] = h.astype(BF16)
        if with_ht:
            hto_ref[...] = h.T.astype(BF16)


def _residual_ln(x, y, mod, layer, gate_chunk, ln_g, ln_b, row, alpha, nxt, with_ht=False):
    t, d = x.shape
    tr = min(256, t)
    with_h = nxt is not None
    in_specs = [pl.BlockSpec((tr, d), lambda i: (i, 0)),
                pl.BlockSpec((tr, d), lambda i: (i, 0)),
                _mod_spec(d, layer, gate_chunk),
                pl.BlockSpec((1, d), lambda i: (0, 0)),
                pl.BlockSpec((1, d), lambda i: (0, 0))]
    args = [x, y, mod, ln_g.reshape(1, d), ln_b.reshape(1, d)]
    out_specs = [pl.BlockSpec((tr, d), lambda i: (i, 0))]
    out_shape = [jax.ShapeDtypeStruct((t, d), F32)]
    if with_h:
        in_specs += [_mod_spec(d, nxt[0], nxt[1]), _mod_spec(d, nxt[0], nxt[2])]
        args += [mod, mod]
        out_specs.append(pl.BlockSpec((tr, d), lambda i: (i, 0)))
        out_shape.append(jax.ShapeDtypeStruct((t, d), BF16))
    if with_ht:
        out_specs.append(pl.BlockSpec((d, tr), lambda i: (0, i)))
        out_shape.append(jax.ShapeDtypeStruct((d, t), BF16))
    outs = pl.pallas_call(
        functools.partial(_ln_kernel, row=row, alpha=alpha, with_h=with_h, with_ht=with_ht),
        grid=(t // tr,),
        in_specs=in_specs, out_specs=out_specs, out_shape=out_shape,
        compiler_params=_cparams(("arbitrary",), 48),
        name="residual_ln",
    )(*args)
    outs = list(outs) + [None] * (3 - len(outs))
    return outs[0], outs[1], outs[2]


def _mm_kernel(a_ref, w_ref, o_ref):
    o_ref[...] = jnp.dot(a_ref[...], w_ref[...], preferred_element_type=F32).astype(o_ref.dtype)


def _matmul(a, w, out_dtype):
    m, k = a.shape
    _, n = w.shape
    tm = _tile(m, 1024, SUBLANE)
    tn = _tile(n, 1024, LANE)
    return pl.pallas_call(
        _mm_kernel,
        grid=(m // tm, n // tn),
        in_specs=[pl.BlockSpec((tm, k), lambda i, j: (i, 0)),
                  pl.BlockSpec((k, tn), lambda i, j: (0, j))],
        out_specs=pl.BlockSpec((tm, tn), lambda i, j: (i, j)),
        out_shape=jax.ShapeDtypeStruct((m, n), out_dtype),
        compiler_params=_cparams(("arbitrary", "arbitrary"), 48),
        name="matmul",
    )(a, w)


def _cast_kernel(w_ref, o_ref):
    o_ref[...] = w_ref[0].astype(BF16)


def _to_bf16(w_stack, idx):
    _, r, c = w_stack.shape
    block_bytes = 8 * 1024 * 1024
    tr = _tile(r, max(16, block_bytes // (4 * c)), 16)
    return pl.pallas_call(
        _cast_kernel,
        grid=(r // tr,),
        in_specs=[pl.BlockSpec((1, tr, c), lambda i: (idx, i, 0))],
        out_specs=pl.BlockSpec((tr, c), lambda i: (i, 0)),
        out_shape=jax.ShapeDtypeStruct((r, c), BF16),
        compiler_params=_cparams(("arbitrary",), 32),
        name="to_bf16",
    )(w_stack)


def _bias_kernel(rpb_ref, o_ref):
    h = pl.program_id(0)
    n_dr = 2 * WIN_R - 1
    n_dc = 2 * WIN_C - 1
    w = lax.broadcasted_iota(jnp.int32, (GRID_W, GRID_W), 0)
    col = lax.broadcasted_iota(jnp.int32, (GRID_W, GRID_W), 1)
    dcm = col - w + (WIN_C - 1)
    cs = jnp.clip(w - WIN_C // 2, 0, GRID_W - WIN_C)
    valid = (col >= cs) & (col < cs + WIN_C)
    planes = []
    for dr in range(n_dr):
        acc = jnp.full((GRID_W, GRID_W), NEG, F32)
        for dc in range(n_dc):
            acc = jnp.where(dcm == dc, rpb_ref[(h * n_dr + dr) * n_dc + dc], acc)
        planes.append(jnp.where(valid, acc, NEG))
    for off in range(WIN_R):
        o_ref[0, off] = jnp.concatenate([planes[off + a] for a in range(WIN_R)], axis=1)


def _bias_table(rpb):
    heads = rpb.shape[0]
    return pl.pallas_call(
        _bias_kernel,
        grid=(heads,),
        in_specs=[pl.BlockSpec(memory_space=pltpu.SMEM)],
        out_specs=pl.BlockSpec((1, WIN_R, GRID_W, WIN_R * GRID_W), lambda h: (h, 0, 0, 0)),
        out_shape=jax.ShapeDtypeStruct((heads, WIN_R, GRID_W, WIN_R * GRID_W), F32),
        compiler_params=_cparams(("arbitrary",), 32),
        name="na_bias_table",
    )(rpb.reshape(-1))


def _na_kernel(q_ref, k_ref, v_ref, kc_ref, vc_ref, tb_ref, o_ref, *, rows_per_step, n_rows, scale):
    blk = pl.program_id(1)
    win = WIN_R * GRID_W
    nt = (((1,), (1,)), ((), ()))
    heads = range(GROUP_LANES // HEAD_DIM)
    lanes = [slice(h * HEAD_DIM, (h + 1) * HEAD_DIM) for h in heads]
    rows, kstarts, offs = [], [], []
    for rl in range(rows_per_step):
        r = blk * rows_per_step + rl
        rs = jnp.clip(r - WIN_R // 2, 0, n_rows - WIN_R)
        offs.append(rs - r + (WIN_R - 1))
        kstarts.append(pl.multiple_of(rs * GRID_W, GRID_W))
        rows.append(slice(rl * GRID_W, (rl + 1) * GRID_W))
    units = [(h, i) for h in heads for i in range(rows_per_step)]

    q_all = [q_ref[:, lanes[h]] for h in heads]
    s_ctx = [lax.dot_general(q_all[h], kc_ref[:, lanes[h]], nt, preferred_element_type=F32) * scale
             for h in heads]
    m_ctx = [jnp.max(s_ctx[h], axis=-1, keepdims=True) for h in heads]
    s_loc, m, p, l_loc, o_loc = {}, {}, {}, {}, {}

    def stage(k, u):
        h, i = u
        if k == 0:
            s_loc[u] = lax.dot_general(q_all[h][rows[i]], k_ref[pl.ds(kstarts[i], win), lanes[h]],
                                       nt, preferred_element_type=F32) * scale + tb_ref[h, offs[i]]
        elif k == 1:
            m[u] = jnp.maximum(jnp.max(s_loc[u], axis=-1, keepdims=True), m_ctx[h][rows[i]])
        elif k == 2:
            p[u] = jnp.exp(s_loc[u] - m[u])
        elif k == 3:
            l_loc[u] = jnp.sum(p[u], axis=-1, keepdims=True)
        else:
            o_loc[u] = jnp.dot(p[u].astype(BF16), v_ref[pl.ds(kstarts[i], win), lanes[h]],
                               preferred_element_type=F32)

    n_stages = 5
    for tick in range(len(units) + n_stages - 1):
        for k in range(n_stages):
            j = tick - k
            if 0 <= j < len(units):
                stage(k, units[j])

    for h in heads:
        col = lambda d: jnp.concatenate([d[h, i] for i in range(rows_per_step)], axis=0)
        p_ctx = jnp.exp(s_ctx[h] - col(m))
        denom = col(l_loc) + jnp.sum(p_ctx, axis=-1, keepdims=True)
        o = col(o_loc) + jnp.dot(p_ctx.astype(BF16), vc_ref[:, lanes[h]],
                                 preferred_element_type=F32)
        o_ref[:, lanes[h]] = (o / denom).astype(BF16)


def _na_attention(qkv, qkv_ctx, bias_tab):
    t, d3 = qkv.shape
    d = d3 // 3
    n_ctx = qkv_ctx.shape[0]
    n_rows = t // GRID_W
    rows_per_step = 8
    groups = d // GROUP_LANES
    tq = rows_per_step * GRID_W
    hg = GROUP_LANES // HEAD_DIM
    kern = functools.partial(_na_kernel, rows_per_step=rows_per_step, n_rows=n_rows,
                             scale=HEAD_DIM ** -0.5)
    return pl.pallas_call(
        kern,
        grid=(groups, n_rows // rows_per_step),
        in_specs=[pl.BlockSpec((tq, GROUP_LANES), lambda g, b: (b, g)),
                  pl.BlockSpec((t, GROUP_LANES), lambda g, b: (0, groups + g)),
                  pl.BlockSpec((t, GROUP_LANES), lambda g, b: (0, 2 * groups + g)),
                  pl.BlockSpec((n_ctx, GROUP_LANES), lambda g, b: (0, groups + g)),
                  pl.BlockSpec((n_ctx, GROUP_LANES), lambda g, b: (0, 2 * groups + g)),
                  pl.BlockSpec((hg, WIN_R, GRID_W, WIN_R * GRID_W), lambda g, b: (g, 0, 0, 0))],
        out_specs=pl.BlockSpec((tq, GROUP_LANES), lambda g, b: (b, g)),
        out_shape=jax.ShapeDtypeStruct((t, d), BF16),
        compiler_params=_cparams(("arbitrary", "arbitrary"), 48),
        name="na_attention",
    )(qkv, qkv, qkv, qkv_ctx, qkv_ctx, bias_tab)


def _ctx_attn_kernel(q_ref, k_ref, v_ref, o_ref, *, scale):
    nt = (((1,), (1,)), ((), ()))
    for hl in range(GROUP_LANES // HEAD_DIM):
        lanes = slice(hl * HEAD_DIM, (hl + 1) * HEAD_DIM)
        s = lax.dot_general(q_ref[:, lanes], k_ref[:, lanes], nt, preferred_element_type=F32) * scale
        m = jnp.max(s, axis=-1, keepdims=True)
        p = jnp.exp(s - m)
        denom = jnp.sum(p, axis=-1, keepdims=True)
        o = jnp.dot(p.astype(BF16), v_ref[:, lanes], preferred_element_type=F32)
        o_ref[:, lanes] = (o / denom).astype(BF16)


def _ctx_attention(qkv_ctx):
    n_ctx, d3 = qkv_ctx.shape
    d = d3 // 3
    groups = d // GROUP_LANES
    return pl.pallas_call(
        functools.partial(_ctx_attn_kernel, scale=HEAD_DIM ** -0.5),
        grid=(groups,),
        in_specs=[pl.BlockSpec((n_ctx, GROUP_LANES), lambda g: (0, g)),
                  pl.BlockSpec((n_ctx, GROUP_LANES), lambda g: (0, groups + g)),
                  pl.BlockSpec((n_ctx, GROUP_LANES), lambda g: (0, 2 * groups + g))],
        out_specs=pl.BlockSpec((n_ctx, GROUP_LANES), lambda g: (0, g)),
        out_shape=jax.ShapeDtypeStruct((n_ctx, d), BF16),
        compiler_params=_cparams(("arbitrary",), 32),
        name="ctx_attention",
    )(qkv_ctx, qkv_ctx, qkv_ctx)


def _scan_block(a, u, h_in, reverse):
    tb, c = a.shape
    ng = tb // SUBLANE
    a3 = a.reshape(ng, SUBLANE, c)
    u3 = u.reshape(ng, SUBLANE, c)
    j = lax.broadcasted_iota(jnp.int32, (ng, SUBLANE, c), 1)
    d = 1
    while d < SUBLANE:
        if reverse:
            keep = j < SUBLANE - d
            shift = SUBLANE - d
        else:
            keep = j >= d
            shift = d
        a_s = jnp.where(keep, pltpu.roll(a3, shift, 1), 1.0)
        u_s = jnp.where(keep, pltpu.roll(u3, shift, 1), 0.0)
        u3 = u3 + a3 * u_s
        a3 = a3 * a_s
        d *= 2
    hs = [None] * ng
    h = h_in
    order = range(ng - 1, -1, -1) if reverse else range(ng)
    edge = 0 if reverse else SUBLANE - 1
    for g in order:
        hg = u3[g] + a3[g] * h
        hs[g] = hg
        h = hg[edge:edge + 1, :]
    return jnp.concatenate(hs, axis=0), h


def _rg_kernel(curf_ref, prevf_ref, nextf_ref, curb_ref, prevb_ref, nextb_ref,
               cw_ref, cb_ref, wg_ref, br_ref, bi_ref, lam_ref, h0_ref,
               hsf_ref, hsb_ref, hfin_ref, xbuf_ref, carry_ref, *, tb, n_tb):
    i = pl.program_id(1)

    @pl.when(i == 0)
    def _():
        carry_ref[...] = h0_ref[...]

    cw = cw_ref[...]
    cb = cb_ref[...]
    halo = SUBLANE

    def gates(cur_ref, prev_ref, next_ref, blk, e):
        prev = jnp.where(blk == 0, 0.0, prev_ref[...])
        nxt = jnp.where(blk == n_tb - 1, 0.0, next_ref[...])
        xbuf_ref[e, 0:halo, :] = prev
        xbuf_ref[e, halo:halo + tb, :] = cur_ref[...]
        xbuf_ref[e, halo + tb:2 * halo + tb, :] = nxt
        left = CONV_W // 2
        xc = cb
        for k in range(CONV_W):
            xc = xc + xbuf_ref[e, halo - left + k:halo - left + k + tb, :] * cw[k:k + 1, :]
        c = xc.shape[1]
        z = jnp.dot(xc.astype(BF16), wg_ref[0, :, 2 * c * e:2 * c * (e + 1)],
                    preferred_element_type=F32)
        r = _sigmoid(z[:, :c] + br_ref[e:e + 1, :])
        ig = _sigmoid(z[:, c:] + bi_ref[e:e + 1, :])
        lam = lam_ref[e:e + 1, :]
        softplus = jnp.maximum(-lam, 0.0) + jnp.log(1.0 + jnp.exp(-jnp.abs(lam)))
        log_a = (-RG_C) * r * softplus
        a = jnp.exp(log_a)
        u = jnp.sqrt(1.0 - jnp.exp(2.0 * log_a)) * ig * xc
        return a, u

    a_f, u_f = gates(curf_ref, prevf_ref, nextf_ref, i, 0)
    h_f, last_f = _scan_block(a_f, u_f, carry_ref[0:1, :], reverse=False)
    hsf_ref[...] = h_f
    carry_ref[0:1, :] = last_f

    a_b, u_b = gates(curb_ref, prevb_ref, nextb_ref, n_tb - 1 - i, 1)
    h_b, last_b = _scan_block(a_b, u_b, carry_ref[1:2, :], reverse=True)
    hsb_ref[...] = h_b
    carry_ref[1:2, :] = last_b

    @pl.when(i == n_tb - 1)
    def _():
        hfin_ref[...] = carry_ref[...]


def _rg_scan(proj, conv_w, conv_b, w_gate, b_r, b_i, lam, h0):
    t, d2 = proj.shape
    d = d2 // 2
    c = MXU_DIM
    nb = d // c
    tb = min(512, t)
    n_tb = t // tb
    hb = tb // SUBLANE
    n_halo = t // SUBLANE

    def cur_f(n, i): return (i, nb + n)
    def prev_f(n, i): return (jnp.maximum(i * hb - 1, 0), nb + n)
    def next_f(n, i): return (jnp.minimum((i + 1) * hb, n_halo - 1), nb + n)
    def cur_b(n, i): return (n_tb - 1 - i, nb + n)
    def prev_b(n, i): return (jnp.maximum((n_tb - 1 - i) * hb - 1, 0), nb + n)
    def next_b(n, i): return (jnp.minimum((n_tb - i) * hb, n_halo - 1), nb + n)

    row = lambda n, i: (0, n)
    kern = functools.partial(_rg_kernel, tb=tb, n_tb=n_tb)
    return pl.pallas_call(
        kern,
        grid=(nb, n_tb),
        in_specs=[pl.BlockSpec((tb, c), cur_f), pl.BlockSpec((SUBLANE, c), prev_f),
                  pl.BlockSpec((SUBLANE, c), next_f),
                  pl.BlockSpec((tb, c), cur_b), pl.BlockSpec((SUBLANE, c), prev_b),
                  pl.BlockSpec((SUBLANE, c), next_b),
                  pl.BlockSpec((CONV_W, c), row), pl.BlockSpec((1, c), row),
                  pl.BlockSpec((1, c, 4 * c), lambda n, i: (n, 0, 0)),
                  pl.BlockSpec((2, c), row), pl.BlockSpec((2, c), row), pl.BlockSpec((2, c), row),
                  pl.BlockSpec((SUBLANE, c), row)],
        out_specs=[pl.BlockSpec((tb, c), lambda n, i: (i, n)),
                   pl.BlockSpec((tb, c), lambda n, i: (n_tb - 1 - i, n)),
                   pl.BlockSpec((SUBLANE, c), row)],
        out_shape=[jax.ShapeDtypeStruct((t, d), F32), jax.ShapeDtypeStruct((t, d), F32),
                   jax.ShapeDtypeStruct((SUBLANE, d), F32)],
        scratch_shapes=[pltpu.VMEM((2, tb + 2 * SUBLANE, c), F32), pltpu.VMEM((SUBLANE, c), F32)],
        compiler_params=_cparams(("arbitrary", "arbitrary"), 32),
        name="rg_scan",
    )(proj, proj, proj, proj, proj, proj, conv_w, conv_b.reshape(1, d), w_gate, b_r, b_i, lam, h0)


def _rg_combine_kernel(hf_ref, hb_ref, g_ref, o_ref):
    o_ref[...] = ((hf_ref[...] + hb_ref[...]) * _gelu_tanh(g_ref[...])).astype(BF16)


def _rg_combine(hs_f, hs_b, proj):
    t, d = hs_f.shape
    tr = min(256, t)
    spec = pl.BlockSpec((tr, d), lambda i: (i, 0))
    return pl.pallas_call(
        _rg_combine_kernel,
        grid=(t // tr,),
        in_specs=[spec, spec, spec],
        out_specs=spec,
        out_shape=jax.ShapeDtypeStruct((t, d), BF16),
        compiler_params=_cparams(("arbitrary",), 40),
        name="rg_combine",
    )(hs_f, hs_b, proj)


def _top_values(s, k):
    out = []
    for _ in range(k):
        m = jnp.max(s, axis=0, keepdims=True)
        s = jnp.where(s == m, -jnp.inf, s)
        out.append(m)
    return out


def _route_head(s1, s2):
    k = PEER_TOPK
    top1 = _top_values(s1, k)
    top2 = _top_values(s2, k)
    b16 = jnp.concatenate(top2, axis=0)
    b8 = b16[:SUBLANE]
    rank = lax.broadcasted_iota(jnp.int32, b8.shape, 0)
    pieces = [top1[0] + b16, top1[1] + b8]
    for a in range(2, SUBLANE):
        pieces.append(jnp.where(rank < k // (a + 1), top1[a] + b8, -jnp.inf))
    pieces.append(jnp.concatenate(top1[SUBLANE:], axis=0) + top2[0])
    cand = jnp.concatenate(pieces, axis=0)
    tau = _top_values(cand, k)[-1]
    m1, m2 = top1[0], top2[0]
    z = jnp.sum(jnp.where(cand >= tau, jnp.exp(cand - (m1 + m2)), 0.0), axis=0, keepdims=True)
    return tau, m1, m2, z


def _route_kernel(q_ref, keys_ref, s_ref, e_ref, tau_ref, *, heads_per_step):
    nt = (((1,), (1,)), ((), ()))
    half = keys_ref.shape[2]
    k1 = keys_ref[0].astype(BF16)
    k2 = keys_ref[1].astype(BF16)
    for h in range(heads_per_step):
        base = 2 * half * h
        s1 = lax.dot_general(k1, q_ref[:, base:base + half], nt, preferred_element_type=F32)
        s2 = lax.dot_general(k2, q_ref[:, base + half:base + 2 * half], nt,
                             preferred_element_type=F32)
        tau, m1, m2, z = _route_head(s1, s2)
        s_ref[h, 0] = s1
        s_ref[h, 1] = s2
        e_ref[h, 0] = jnp.exp(s1 - m1)
        e_ref[h, 1] = jnp.exp(s2 - m2) / z
        tau_ref[h] = jnp.broadcast_to(tau, tau_ref.shape[1:])


def _peer_route(q, sub_keys):
    t = q.shape[0]
    _, n_keys, half = sub_keys.shape
    tm = min(256, t)
    hps = 4
    return pl.pallas_call(
        functools.partial(_route_kernel, heads_per_step=hps),
        grid=(t // tm, PEER_HEADS // hps),
        in_specs=[pl.BlockSpec((tm, hps * 2 * half), lambda i, h: (i, h)),
                  pl.BlockSpec((2, n_keys, half), lambda i, h: (0, 0, 0))],
        out_specs=[pl.BlockSpec((hps, 2, n_keys, tm), lambda i, h: (h, 0, 0, i)),
                   pl.BlockSpec((hps, 2, n_keys, tm), lambda i, h: (h, 0, 0, i)),
                   pl.BlockSpec((hps, SUBLANE, tm), lambda i, h: (h, 0, i))],
        out_shape=[jax.ShapeDtypeStruct((PEER_HEADS, 2, n_keys, t), F32),
                   jax.ShapeDtypeStruct((PEER_HEADS, 2, n_keys, t), F32),
                   jax.ShapeDtypeStruct((PEER_HEADS, SUBLANE, t), F32)],
        compiler_params=_cparams(("arbitrary", "arbitrary"), 32),
        name="peer_route",
    )(q, sub_keys)


def _expert_kernel(x_ref, u_ref, v_ref, s1_ref, e1_ref, s2_ref, e2_ref, tau_ref, o_ref, acc_ref,
                   *, n_keys, rows_per_chunk):
    c = pl.program_id(1)
    tm = x_ref.shape[1]

    @pl.when(c == 0)
    def _():
        acc_ref[...] = jnp.zeros(acc_ref.shape, F32)

    act = jnp.dot(u_ref[...], x_ref[...], preferred_element_type=F32)
    pieces = []
    for il in range(rows_per_chunk):
        parts = []
        for lb in range(tm // LANE):
            ln = slice(lb * LANE, (lb + 1) * LANE)
            a = act[il * n_keys:(il + 1) * n_keys, ln]
            z = jnp.where(a[:1] != a[:1], 1.0, 0.0)
            g = jnp.zeros((n_keys, LANE), F32)
            for h in range(PEER_HEADS):
                ssum = s1_ref[h, 0, il:il + 1, ln] + s2_ref[h, 0, :, ln]
                gate = e1_ref[h, 0, il:il + 1, ln] * e2_ref[h, 0, :, ln]
                g = g + jnp.where(ssum >= tau_ref[h, 0:1, ln] + z, gate, 0.0)
            parts.append((g * _gelu_tanh(a)).astype(BF16))
        pieces.append(jnp.concatenate(parts, axis=1))
    w_t = jnp.concatenate(pieces, axis=0)
    acc_ref[...] += lax.dot_general(w_t, v_ref[...], (((0,), (0,)), ((), ())),
                                    preferred_element_type=F32)

    @pl.when(c == pl.num_programs(1) - 1)
    def _():
        o_ref[...] = acc_ref[...].astype(o_ref.dtype)


def _peer_experts(x_t, u_tab, v_tab, s, e, tau):
    d, t = x_t.shape
    n_exp = u_tab.shape[0]
    n_keys = s.shape[2]
    rows_per_chunk = SUBLANE
    ce = rows_per_chunk * n_keys
    tm = min(512, t)
    kern = functools.partial(_expert_kernel, n_keys=n_keys, rows_per_chunk=rows_per_chunk)
    return pl.pallas_call(
        kern,
        grid=(t // tm, n_exp // ce),
        in_specs=[pl.BlockSpec((d, tm), lambda i, c: (0, i), pipeline_mode=pl.Buffered(1)),
                  pl.BlockSpec((ce, d), lambda i, c: (c, 0)),
                  pl.BlockSpec((ce, d), lambda i, c: (c, 0)),
                  pl.BlockSpec((PEER_HEADS, 1, rows_per_chunk, tm), lambda i, c: (0, 0, c, i)),
                  pl.BlockSpec((PEER_HEADS, 1, rows_per_chunk, tm), lambda i, c: (0, 0, c, i)),
                  pl.BlockSpec((PEER_HEADS, 1, n_keys, tm), lambda i, c: (0, 1, 0, i)),
                  pl.BlockSpec((PEER_HEADS, 1, n_keys, tm), lambda i, c: (0, 1, 0, i)),
                  pl.BlockSpec((PEER_HEADS, SUBLANE, tm), lambda i, c: (0, 0, i))],
        out_specs=pl.BlockSpec((tm, d), lambda i, c: (i, 0)),
        out_shape=jax.ShapeDtypeStruct((t, d), BF16),
        scratch_shapes=[pltpu.VMEM((tm, d), F32)],
        compiler_params=_cparams(("arbitrary", "arbitrary"), 56),
        name="peer_experts",
    )(x_t, u_tab, v_tab, s, e, s, e, tau)


def _peer(h, h_t, w_q, sub_keys, u_tab, v_tab):
    q = _matmul(h, w_q, BF16)
    s, e, tau = _peer_route(q, sub_keys)
    return _peer_experts(h_t, u_tab, v_tab, s, e, tau)


def kernel(x, c, ctx, c_ctx, w_ada, b_ada, ln_g, ln_b, na_w_qkv, na_w_o, na_rpb, rg_w_in, rg_conv_w, rg_conv_b, rg_w_r, rg_b_r, rg_w_i, rg_b_i, rg_lam, rg_w_out, peer_w_q, peer_sub_keys, peer_u, peer_v):
    assert x.shape[0] == 1 and ctx.shape[0] == 1
    depth = w_ada.shape[0]
    d = x.shape[2]
    alpha = (2 * depth) ** 0.25
    n_mixers = 2
    lat, con = 0, 1

    cond8 = jnp.zeros((SUBLANE, d), F32).at[lat].set(c[0]).at[con].set(c_ctx)
    mod = _modulation(cond8, w_ada, b_ada)

    xl = x[0]
    xc = ctx[0]
    hl = _modulate(xl, mod, 0, lat)
    hc = _modulate(xc, mod, 0, con)

    for l in range(depth):
        ctx_out = l < depth - 1
        j = l // n_mixers
        yc = None
        if l % n_mixers == 0:
            w_qkv = _to_bf16(na_w_qkv, j)
            w_o = _to_bf16(na_w_o, j)
            qkv_l = _matmul(hl, w_qkv, BF16)
            qkv_c = _matmul(hc, w_qkv, BF16)
            bias_tab = _bias_table(na_rpb[j])
            yl = _matmul(_na_attention(qkv_l, qkv_c, bias_tab), w_o, BF16)
            if ctx_out:
                yc = _matmul(_ctx_attention(qkv_c), w_o, BF16)
        else:
            w_in = _to_bf16(rg_w_in, j)
            w_out = _to_bf16(rg_w_out, j)
            nb = rg_w_r.shape[2]
            w_gate = jnp.concatenate([rg_w_r[j, 0], rg_w_i[j, 0], rg_w_r[j, 1], rg_w_i[j, 1]],
                                     axis=-1).astype(BF16)
            assert w_gate.shape[0] == nb
            rg_args = (rg_conv_w[j], rg_conv_b[j], w_gate, rg_b_r[j], rg_b_i[j], rg_lam[j])
            proj_c = _matmul(hc, w_in, F32)
            hs_fc, hs_bc, h_fin = _rg_scan(proj_c, *rg_args, jnp.zeros((SUBLANE, d), F32))
            proj_l = _matmul(hl, w_in, F32)
            hs_f, hs_b, _ = _rg_scan(proj_l, *rg_args, h_fin)
            yl = _matmul(_rg_combine(hs_f, hs_b, proj_l), w_out, BF16)
            if ctx_out:
                yc = _matmul(_rg_combine(hs_fc, hs_bc, proj_c), w_out, BF16)

        w_q = _to_bf16(peer_w_q, l)
        u_tab = _to_bf16(peer_u, l)
        v_tab = _to_bf16(peer_v, l)
        nxt = (l + 1, 1, 0) if ctx_out else None

        xl, hl, hl_t = _residual_ln(xl, yl, mod, l, 2, ln_g[l, 0], ln_b[l, 0], lat, alpha,
                                    (l, 4, 3), with_ht=True)
        y_ffn = _peer(hl, hl_t, w_q, peer_sub_keys[l], u_tab, v_tab)
        xl, hl, _ = _residual_ln(xl, y_ffn, mod, l, 5, ln_g[l, 1], ln_b[l, 1], lat, alpha, nxt)
        if ctx_out:
            xc, hc, hc_t = _residual_ln(xc, yc, mod, l, 2, ln_g[l, 0], ln_b[l, 0], con, alpha,
                                        (l, 4, 3), with_ht=True)
            y_cf = _peer(hc, hc_t, w_q, peer_sub_keys[l], u_tab, v_tab)
            xc, hc, _ = _residual_ln(xc, y_cf, mod, l, 5, ln_g[l, 1], ln_b[l, 1], con, alpha, nxt)
    return xl[None]
```

```python
import functools

import jax
import jax.numpy as jnp
from jax import lax
from jax.experimental import pallas as pl
from jax.experimental.pallas import tpu as pltpu

F32 = jnp.float32
BF16 = jnp.bfloat16

GRID_W = 64
WIN_R = 8
WIN_C = 16
HEAD_DIM = 128
RG_C = 8.0
CONV_W = 4
PEER_HEADS = 8
PEER_TOPK = 16
LN_EPS = 1e-6

V7X_VMEM_BYTES = 64 * 1024 * 1024
LANE = 128
SUBLANE = 8
MXU_DIM = 256

NEG = -1e30
GROUP_LANES = 2 * HEAD_DIM


def _cparams(sem, vmem_mb, flags=None):
    assert vmem_mb * 1024 * 1024 < V7X_VMEM_BYTES
    return pltpu.CompilerParams(dimension_semantics=sem, vmem_limit_bytes=vmem_mb * 1024 * 1024,
                                flags=flags)


def _tile(n, target, unit):
    t = min(target, n) // unit * unit
    while n % t:
        t -= unit
    return t


def _gelu_tanh(x):
    return 0.5 * x * (1.0 + jnp.tanh(0.7978845608028654 * (x + 0.044715 * (x * x * x))))


def _sigmoid(x):
    return 1.0 / (1.0 + jnp.exp(-x))


def _mod_kernel(c_ref, w_ref, b_ref, o_ref):
    c = c_ref[...]
    s = (c * _sigmoid(c)).astype(BF16)
    o_ref[0] = jnp.dot(s, w_ref[0].astype(BF16), preferred_element_type=F32) + b_ref[0]


def _modulation(cond8, w_ada, b_ada):
    depth, d, n = w_ada.shape
    tn = 512
    return pl.pallas_call(
        _mod_kernel,
        grid=(depth, n // tn),
        in_specs=[pl.BlockSpec((SUBLANE, d), lambda l, j: (0, 0)),
                  pl.BlockSpec((1, d, tn), lambda l, j: (l, 0, j)),
                  pl.BlockSpec((1, 1, tn), lambda l, j: (l, 0, j))],
        out_specs=pl.BlockSpec((1, SUBLANE, tn), lambda l, j: (l, 0, j)),
        out_shape=jax.ShapeDtypeStruct((depth, SUBLANE, n), F32),
        compiler_params=_cparams(("arbitrary", "arbitrary"), 40),
        name="modulation",
    )(cond8, w_ada, b_ada.reshape(depth, 1, n))


def _mod_spec(d, layer, chunk):
    return pl.BlockSpec((1, SUBLANE, d), lambda *_: (layer, 0, chunk))


def _modulate_kernel(x_ref, sc_ref, sh_ref, o_ref, *, row):
    sc = sc_ref[0, row:row + 1, :]
    sh = sh_ref[0, row:row + 1, :]
    o_ref[...] = (x_ref[...] * (1.0 + sc) + sh).astype(BF16)


def _modulate(x, mod, layer, row):
    t, d = x.shape
    tr = min(256, t)
    return pl.pallas_call(
        functools.partial(_modulate_kernel, row=row),
        grid=(t // tr,),
        in_specs=[pl.BlockSpec((tr, d), lambda i: (i, 0)),
                  _mod_spec(d, layer, 1), _mod_spec(d, layer, 0)],
        out_specs=pl.BlockSpec((tr, d), lambda i: (i, 0)),
        out_shape=jax.ShapeDtypeStruct((t, d), BF16),
        compiler_params=_cparams(("arbitrary",), 32),
        name="modulate",
    )(x, mod, mod)


def _ln_kernel(x_ref, y_ref, g_ref, lg_ref, lb_ref, *rest, row, alpha, with_h, with_ht):
    hto_ref = None
    if with_ht:
        sc_ref, sh_ref, xo_ref, ho_ref, hto_ref = rest
    elif with_h:
        sc_ref, sh_ref, xo_ref, ho_ref = rest
    else:
        (xo_ref,) = rest
    z = alpha * x_ref[...] + g_ref[0, row:row + 1, :] * y_ref[...].astype(F32)
    mu = jnp.mean(z, axis=-1, keepdims=True)
    zc = z - mu
    var = jnp.mean(zc * zc, axis=-1, keepdims=True)
    xn = zc * lax.rsqrt(var + LN_EPS) * lg_ref[...] + lb_ref[...]
    xo_ref[...] = xn
    if with_h:
        h = xn * (1.0 + sc_ref[0, row:row + 1, :]) + sh_ref[0, row:row + 1, :]
        ho_ref[...] = h.astype(BF16)
        if with_ht:
            hto_ref[...] = h.T.astype(BF16)


def _residual_ln(x, y, mod, layer, gate_chunk, ln_g, ln_b, row, alpha, nxt, with_ht=False):
    t, d = x.shape
    tr = min(256, t)
    with_h = nxt is not None
    in_specs = [pl.BlockSpec((tr, d), lambda i: (i, 0)),
                pl.BlockSpec((tr, d), lambda i: (i, 0)),
                _mod_spec(d, layer, gate_chunk),
                pl.BlockSpec((1, d), lambda i: (0, 0)),
                pl.BlockSpec((1, d), lambda i: (0, 0))]
    args = [x, y, mod, ln_g.reshape(1, d), ln_b.reshape(1, d)]
    out_specs = [pl.BlockSpec((tr, d), lambda i: (i, 0))]
    out_shape = [jax.ShapeDtypeStruct((t, d), F32)]
    if with_h:
        in_specs += [_mod_spec(d, nxt[0], nxt[1]), _mod_spec(d, nxt[0], nxt[2])]
        args += [mod, mod]
        out_specs.append(pl.BlockSpec((tr, d), lambda i: (i, 0)))
        out_shape.append(jax.ShapeDtypeStruct((t, d), BF16))
    if with_ht:
        out_specs.append(pl.BlockSpec((d, tr), lambda i: (0, i)))
        out_shape.append(jax.ShapeDtypeStruct((d, t), BF16))
    outs = pl.pallas_call(
        functools.partial(_ln_kernel, row=row, alpha=alpha, with_h=with_h, with_ht=with_ht),
        grid=(t // tr,),
        in_specs=in_specs, out_specs=out_specs, out_shape=out_shape,
        compiler_params=_cparams(("arbitrary",), 48),
        name="residual_ln",
    )(*args)
    outs = list(outs) + [None] * (3 - len(outs))
    return outs[0], outs[1], outs[2]


def _mm_kernel(a_ref, w_ref, o_ref):
    o_ref[...] = jnp.dot(a_ref[...], w_ref[...], preferred_element_type=F32).astype(o_ref.dtype)


def _matmul(a, w, out_dtype):
    m, k = a.shape
    _, n = w.shape
    tm = _tile(m, 1024, SUBLANE)
    tn = _tile(n, 1024, LANE)
    return pl.pallas_call(
        _mm_kernel,
        grid=(m // tm, n // tn),
        in_specs=[pl.BlockSpec((tm, k), lambda i, j: (i, 0)),
                  pl.BlockSpec((k, tn), lambda i, j: (0, j))],
        out_specs=pl.BlockSpec((tm, tn), lambda i, j: (i, j)),
        out_shape=jax.ShapeDtypeStruct((m, n), out_dtype),
        compiler_params=_cparams(("arbitrary", "arbitrary"), 48),
        name="matmul",
    )(a, w)


def _proj_ln_kernel(a_ref, w_ref, x_ref, g_ref, lg_ref, lb_ref, sc_ref, sh_ref,
                    xo_ref, ho_ref, hto_ref, *, row, alpha):
    y = jnp.dot(a_ref[...], w_ref[...], preferred_element_type=F32)
    z = alpha * x_ref[...] + g_ref[0, row:row + 1, :] * y
    mu = jnp.mean(z, axis=-1, keepdims=True)
    zc = z - mu
    var = jnp.mean(zc * zc, axis=-1, keepdims=True)
    xn = zc * lax.rsqrt(var + LN_EPS) * lg_ref[...] + lb_ref[...]
    xo_ref[...] = xn
    h = xn * (1.0 + sc_ref[0, row:row + 1, :]) + sh_ref[0, row:row + 1, :]
    ho_ref[...] = h.astype(BF16)
    hto_ref[...] = h.T.astype(BF16)


def _proj_residual_ln(a, w, x, mod, layer, gate_chunk, ln_g, ln_b, row, alpha, nxt):
    t, k = a.shape
    d = w.shape[1]
    tr = min(128, t)
    rows = lambda i: (i, 0)
    fixed = lambda i: (0, 0)
    return pl.pallas_call(
        functools.partial(_proj_ln_kernel, row=row, alpha=alpha),
        grid=(t // tr,),
        in_specs=[pl.BlockSpec((tr, k), rows),
                  pl.BlockSpec((k, d), fixed, pipeline_mode=pl.Buffered(1)),
                  pl.BlockSpec((tr, d), rows),
                  _mod_spec(d, layer, gate_chunk),
                  pl.BlockSpec((1, d), fixed), pl.BlockSpec((1, d), fixed),
                  _mod_spec(d, nxt[0], nxt[1]), _mod_spec(d, nxt[0], nxt[2])],
        out_specs=[pl.BlockSpec((tr, d), rows), pl.BlockSpec((tr, d), rows),
                   pl.BlockSpec((d, tr), lambda i: (0, i))],
        out_shape=[jax.ShapeDtypeStruct((t, d), F32), jax.ShapeDtypeStruct((t, d), BF16),
                   jax.ShapeDtypeStruct((d, t), BF16)],
        compiler_params=_cparams(("arbitrary",), 56),
        name="proj_residual_ln",
    )(a, w, x, mod, ln_g.reshape(1, d), ln_b.reshape(1, d), mod, mod)


def _cast_kernel(w_ref, o_ref):
    o_ref[...] = w_ref[0].astype(BF16)


def _to_bf16(w_stack, idx):
    _, r, c = w_stack.shape
    block_bytes = 8 * 1024 * 1024
    tr = _tile(r, max(16, block_bytes // (4 * c)), 16)
    return pl.pallas_call(
        _cast_kernel,
        grid=(r // tr,),
        in_specs=[pl.BlockSpec((1, tr, c), lambda i: (idx, i, 0))],
        out_specs=pl.BlockSpec((tr, c), lambda i: (i, 0)),
        out_shape=jax.ShapeDtypeStruct((r, c), BF16),
        compiler_params=_cparams(("arbitrary",), 32),
        name="to_bf16",
    )(w_stack)


def _bias_kernel(rpb_ref, o_ref):
    h = pl.program_id(0)
    n_dr = 2 * WIN_R - 1
    n_dc = 2 * WIN_C - 1
    w = lax.broadcasted_iota(jnp.int32, (GRID_W, GRID_W), 0)
    col = lax.broadcasted_iota(jnp.int32, (GRID_W, GRID_W), 1)
    dcm = col - w + (WIN_C - 1)
    cs = jnp.clip(w - WIN_C // 2, 0, GRID_W - WIN_C)
    valid = (col >= cs) & (col < cs + WIN_C)
    planes = []
    for dr in range(n_dr):
        acc = jnp.full((GRID_W, GRID_W), NEG, F32)
        for dc in range(n_dc):
            acc = jnp.where(dcm == dc, rpb_ref[(h * n_dr + dr) * n_dc + dc], acc)
        planes.append(jnp.where(valid, acc, NEG))
    for off in range(WIN_R):
        o_ref[0, off] = jnp.concatenate([planes[off + a] for a in range(WIN_R)], axis=1)


def _bias_table(rpb):
    heads = rpb.shape[0]
    return pl.pallas_call(
        _bias_kernel,
        grid=(heads,),
        in_specs=[pl.BlockSpec(memory_space=pltpu.SMEM)],
        out_specs=pl.BlockSpec((1, WIN_R, GRID_W, WIN_R * GRID_W), lambda h: (h, 0, 0, 0)),
        out_shape=jax.ShapeDtypeStruct((heads, WIN_R, GRID_W, WIN_R * GRID_W), F32),
        compiler_params=_cparams(("arbitrary",), 32),
        name="na_bias_table",
    )(rpb.reshape(-1))


def _na_kernel(q_ref, k_ref, v_ref, kc_ref, vc_ref, tb_ref, o_ref, *, rows_per_step, n_rows, scale):
    blk = pl.program_id(1)
    win = WIN_R * GRID_W
    nt = (((1,), (1,)), ((), ()))
    heads = range(GROUP_LANES // HEAD_DIM)
    lanes = [slice(h * HEAD_DIM, (h + 1) * HEAD_DIM) for h in heads]
    rows, kstarts, offs = [], [], []
    for rl in range(rows_per_step):
        r = blk * rows_per_step + rl
        rs = jnp.clip(r - WIN_R // 2, 0, n_rows - WIN_R)
        offs.append(rs - r + (WIN_R - 1))
        kstarts.append(pl.multiple_of(rs * GRID_W, GRID_W))
        rows.append(slice(rl * GRID_W, (rl + 1) * GRID_W))
    units = [(h, i) for h in heads for i in range(rows_per_step)]

    q_all = [q_ref[:, lanes[h]] for h in heads]
    s_ctx = [lax.dot_general(q_all[h], kc_ref[:, lanes[h]], nt, preferred_element_type=F32) * scale
             for h in heads]
    m_ctx = [jnp.max(s_ctx[h], axis=-1, keepdims=True) for h in heads]
    s_loc, m, p, l_loc, o_loc = {}, {}, {}, {}, {}

    def stage(k, u):
        h, i = u
        if k == 0:
            s_loc[u] = lax.dot_general(q_all[h][rows[i]], k_ref[pl.ds(kstarts[i], win), lanes[h]],
                                       nt, preferred_element_type=F32) * scale + tb_ref[h, offs[i]]
        elif k == 1:
            m[u] = jnp.maximum(jnp.max(s_loc[u], axis=-1, keepdims=True), m_ctx[h][rows[i]])
        elif k == 2:
            p[u] = jnp.exp(s_loc[u] - m[u])
        elif k == 3:
            l_loc[u] = jnp.sum(p[u], axis=-1, keepdims=True)
        else:
            o_loc[u] = jnp.dot(p[u].astype(BF16), v_ref[pl.ds(kstarts[i], win), lanes[h]],
                               preferred_element_type=F32)

    n_stages = 5
    for tick in range(len(units) + n_stages - 1):
        for k in range(n_stages):
            j = tick - k
            if 0 <= j < len(units):
                stage(k, units[j])

    for h in heads:
        col = lambda d: jnp.concatenate([d[h, i] for i in range(rows_per_step)], axis=0)
        p_ctx = jnp.exp(s_ctx[h] - col(m))
        denom = col(l_loc) + jnp.sum(p_ctx, axis=-1, keepdims=True)
        o = col(o_loc) + jnp.dot(p_ctx.astype(BF16), vc_ref[:, lanes[h]],
                                 preferred_element_type=F32)
        o_ref[:, lanes[h]] = (o / denom).astype(BF16)


def _na_attention(qkv, qkv_ctx, bias_tab):
    t, d3 = qkv.shape
    d = d3 // 3
    n_ctx = qkv_ctx.shape[0]
    n_rows = t // GRID_W
    rows_per_step = 16
    groups = d // GROUP_LANES
    tq = rows_per_step * GRID_W
    hg = GROUP_LANES // HEAD_DIM
    kern = functools.partial(_na_kernel, rows_per_step=rows_per_step, n_rows=n_rows,
                             scale=HEAD_DIM ** -0.5)
    return pl.pallas_call(
        kern,
        grid=(groups, n_rows // rows_per_step),
        in_specs=[pl.BlockSpec((tq, GROUP_LANES), lambda g, b: (b, g)),
                  pl.BlockSpec((t, GROUP_LANES), lambda g, b: (0, groups + g)),
                  pl.BlockSpec((t, GROUP_LANES), lambda g, b: (0, 2 * groups + g)),
                  pl.BlockSpec((n_ctx, GROUP_LANES), lambda g, b: (0, groups + g)),
                  pl.BlockSpec((n_ctx, GROUP_LANES), lambda g, b: (0, 2 * groups + g)),
                  pl.BlockSpec((hg, WIN_R, GRID_W, WIN_R * GRID_W), lambda g, b: (g, 0, 0, 0))],
        out_specs=pl.BlockSpec((tq, GROUP_LANES), lambda g, b: (b, g)),
        out_shape=jax.ShapeDtypeStruct((t, d), BF16),
        compiler_params=_cparams(("arbitrary", "arbitrary"), 48),
        name="na_attention",
    )(qkv, qkv, qkv, qkv_ctx, qkv_ctx, bias_tab)


def _ctx_attn_kernel(q_ref, k_ref, v_ref, o_ref, *, scale):
    nt = (((1,), (1,)), ((), ()))
    for hl in range(GROUP_LANES // HEAD_DIM):
        lanes = slice(hl * HEAD_DIM, (hl + 1) * HEAD_DIM)
        s = lax.dot_general(q_ref[:, lanes], k_ref[:, lanes], nt, preferred_element_type=F32) * scale
        m = jnp.max(s, axis=-1, keepdims=True)
        p = jnp.exp(s - m)
        denom = jnp.sum(p, axis=-1, keepdims=True)
        o = jnp.dot(p.astype(BF16), v_ref[:, lanes], preferred_element_type=F32)
        o_ref[:, lanes] = (o / denom).astype(BF16)


def _ctx_attention(qkv_ctx):
    n_ctx, d3 = qkv_ctx.shape
    d = d3 // 3
    groups = d // GROUP_LANES
    return pl.pallas_call(
        functools.partial(_ctx_attn_kernel, scale=HEAD_DIM ** -0.5),
        grid=(groups,),
        in_specs=[pl.BlockSpec((n_ctx, GROUP_LANES), lambda g: (0, g)),
                  pl.BlockSpec((n_ctx, GROUP_LANES), lambda g: (0, groups + g)),
                  pl.BlockSpec((n_ctx, GROUP_LANES), lambda g: (0, 2 * groups + g))],
        out_specs=pl.BlockSpec((n_ctx, GROUP_LANES), lambda g: (0, g)),
        out_shape=jax.ShapeDtypeStruct((n_ctx, d), BF16),
        compiler_params=_cparams(("arbitrary",), 32),
        name="ctx_attention",
    )(qkv_ctx, qkv_ctx, qkv_ctx)


def _scan_block(a, u, h_in, reverse):
    tb, c = a.shape
    ng = tb // SUBLANE
    a3 = a.reshape(ng, SUBLANE, c)
    u3 = u.reshape(ng, SUBLANE, c)
    j = lax.broadcasted_iota(jnp.int32, (ng, SUBLANE, c), 1)
    d = 1
    while d < SUBLANE:
        if reverse:
            keep = j < SUBLANE - d
            shift = SUBLANE - d
        else:
            keep = j >= d
            shift = d
        a_s = jnp.where(keep, pltpu.roll(a3, shift, 1), 1.0)
        u_s = jnp.where(keep, pltpu.roll(u3, shift, 1), 0.0)
        u3 = u3 + a3 * u_s
        a3 = a3 * a_s
        d *= 2
    hs = [None] * ng
    h = h_in
    order = range(ng - 1, -1, -1) if reverse else range(ng)
    edge = 0 if reverse else SUBLANE - 1
    for g in order:
        hg = u3[g] + a3[g] * h
        hs[g] = hg
        h = hg[edge:edge + 1, :]
    return jnp.concatenate(hs, axis=0), h


def _rg_kernel(curf_ref, prevf_ref, nextf_ref, curb_ref, prevb_ref, nextb_ref,
               cw_ref, cb_ref, wg_ref, br_ref, bi_ref, lam_ref, h0_ref,
               hsf_ref, hsb_ref, hfin_ref, xbuf_ref, carry_ref, *, tb, n_tb):
    i = pl.program_id(1)

    @pl.when(i == 0)
    def _():
        carry_ref[...] = h0_ref[...]

    cw = cw_ref[...]
    cb = cb_ref[...]
    halo = SUBLANE

    def gates(cur_ref, prev_ref, next_ref, blk, e):
        prev = jnp.where(blk == 0, 0.0, prev_ref[...])
        nxt = jnp.where(blk == n_tb - 1, 0.0, next_ref[...])
        xbuf_ref[e, 0:halo, :] = prev
        xbuf_ref[e, halo:halo + tb, :] = cur_ref[...]
        xbuf_ref[e, halo + tb:2 * halo + tb, :] = nxt
        left = CONV_W // 2
        xc = cb
        for k in range(CONV_W):
            xc = xc + xbuf_ref[e, halo - left + k:halo - left + k + tb, :] * cw[k:k + 1, :]
        c = xc.shape[1]
        z = jnp.dot(xc.astype(BF16), wg_ref[0, :, 2 * c * e:2 * c * (e + 1)],
                    preferred_element_type=F32)
        r = _sigmoid(z[:, :c] + br_ref[e:e + 1, :])
        ig = _sigmoid(z[:, c:] + bi_ref[e:e + 1, :])
        lam = lam_ref[e:e + 1, :]
        softplus = jnp.maximum(-lam, 0.0) + jnp.log(1.0 + jnp.exp(-jnp.abs(lam)))
        log_a = (-RG_C) * r * softplus
        a = jnp.exp(log_a)
        u = jnp.sqrt(1.0 - jnp.exp(2.0 * log_a)) * ig * xc
        return a, u

    a_f, u_f = gates(curf_ref, prevf_ref, nextf_ref, i, 0)
    h_f, last_f = _scan_block(a_f, u_f, carry_ref[0:1, :], reverse=False)
    hsf_ref[...] = h_f
    carry_ref[0:1, :] = last_f

    a_b, u_b = gates(curb_ref, prevb_ref, nextb_ref, n_tb - 1 - i, 1)
    h_b, last_b = _scan_block(a_b, u_b, carry_ref[1:2, :], reverse=True)
    hsb_ref[...] = h_b
    carry_ref[1:2, :] = last_b

    @pl.when(i == n_tb - 1)
    def _():
        hfin_ref[...] = carry_ref[...]


def _rg_scan(proj, conv_w, conv_b, w_gate, b_r, b_i, lam, h0):
    t, d2 = proj.shape
    d = d2 // 2
    c = MXU_DIM
    nb = d // c
    tb = min(512, t)
    n_tb = t // tb
    hb = tb // SUBLANE
    n_halo = t // SUBLANE

    def cur_f(n, i): return (i, nb + n)
    def prev_f(n, i): return (jnp.maximum(i * hb - 1, 0), nb + n)
    def next_f(n, i): return (jnp.minimum((i + 1) * hb, n_halo - 1), nb + n)
    def cur_b(n, i): return (n_tb - 1 - i, nb + n)
    def prev_b(n, i): return (jnp.maximum((n_tb - 1 - i) * hb - 1, 0), nb + n)
    def next_b(n, i): return (jnp.minimum((n_tb - i) * hb, n_halo - 1), nb + n)

    row = lambda n, i: (0, n)
    kern = functools.partial(_rg_kernel, tb=tb, n_tb=n_tb)
    return pl.pallas_call(
        kern,
        grid=(nb, n_tb),
        in_specs=[pl.BlockSpec((tb, c), cur_f), pl.BlockSpec((SUBLANE, c), prev_f),
                  pl.BlockSpec((SUBLANE, c), next_f),
                  pl.BlockSpec((tb, c), cur_b), pl.BlockSpec((SUBLANE, c), prev_b),
                  pl.BlockSpec((SUBLANE, c), next_b),
                  pl.BlockSpec((CONV_W, c), row), pl.BlockSpec((1, c), row),
                  pl.BlockSpec((1, c, 4 * c), lambda n, i: (n, 0, 0)),
                  pl.BlockSpec((2, c), row), pl.BlockSpec((2, c), row), pl.BlockSpec((2, c), row),
                  pl.BlockSpec((SUBLANE, c), row)],
        out_specs=[pl.BlockSpec((tb, c), lambda n, i: (i, n)),
                   pl.BlockSpec((tb, c), lambda n, i: (n_tb - 1 - i, n)),
                   pl.BlockSpec((SUBLANE, c), row)],
        out_shape=[jax.ShapeDtypeStruct((t, d), F32), jax.ShapeDtypeStruct((t, d), F32),
                   jax.ShapeDtypeStruct((SUBLANE, d), F32)],
        scratch_shapes=[pltpu.VMEM((2, tb + 2 * SUBLANE, c), F32), pltpu.VMEM((SUBLANE, c), F32)],
        compiler_params=_cparams(("arbitrary", "arbitrary"), 32),
        name="rg_scan",
    )(proj, proj, proj, proj, proj, proj, conv_w, conv_b.reshape(1, d), w_gate, b_r, b_i, lam, h0)


def _rg_combine_kernel(hf_ref, hb_ref, g_ref, o_ref):
    o_ref[...] = ((hf_ref[...] + hb_ref[...]) * _gelu_tanh(g_ref[...])).astype(BF16)


def _rg_combine(hs_f, hs_b, proj):
    t, d = hs_f.shape
    tr = min(256, t)
    spec = pl.BlockSpec((tr, d), lambda i: (i, 0))
    return pl.pallas_call(
        _rg_combine_kernel,
        grid=(t // tr,),
        in_specs=[spec, spec, spec],
        out_specs=spec,
        out_shape=jax.ShapeDtypeStruct((t, d), BF16),
        compiler_params=_cparams(("arbitrary",), 40),
        name="rg_combine",
    )(hs_f, hs_b, proj)


def _top_values(s, k):
    out = []
    for _ in range(k):
        m = jnp.max(s, axis=0, keepdims=True)
        s = jnp.where(s == m, -jnp.inf, s)
        out.append(m)
    return out


def _route_head(s1, s2):
    k = PEER_TOPK
    top1 = _top_values(s1, k)
    top2 = _top_values(s2, k)
    b16 = jnp.concatenate(top2, axis=0)
    b8 = b16[:SUBLANE]
    rank = lax.broadcasted_iota(jnp.int32, b8.shape, 0)
    pieces = [top1[0] + b16, top1[1] + b8]
    for a in range(2, SUBLANE):
        pieces.append(jnp.where(rank < k // (a + 1), top1[a] + b8, -jnp.inf))
    pieces.append(jnp.concatenate(top1[SUBLANE:], axis=0) + top2[0])
    cand = jnp.concatenate(pieces, axis=0)
    tau = _top_values(cand, k)[-1]
    m1, m2 = top1[0], top2[0]
    z = jnp.sum(jnp.where(cand >= tau, jnp.exp(cand - (m1 + m2)), 0.0), axis=0, keepdims=True)
    return tau, m1, m2, z


def _route_kernel(q_ref, keys_ref, s_ref, e_ref, tau_ref, *, heads_per_step):
    nt = (((1,), (1,)), ((), ()))
    half = keys_ref.shape[2]
    k1 = keys_ref[0].astype(BF16)
    k2 = keys_ref[1].astype(BF16)
    for h in range(heads_per_step):
        base = 2 * half * h
        s1 = lax.dot_general(k1, q_ref[:, base:base + half], nt, preferred_element_type=F32)
        s2 = lax.dot_general(k2, q_ref[:, base + half:base + 2 * half], nt,
                             preferred_element_type=F32)
        tau, m1, m2, z = _route_head(s1, s2)
        s_ref[h, 0] = s1
        s_ref[h, 1] = s2
        e_ref[h, 0] = jnp.exp(s1 - m1)
        e_ref[h, 1] = jnp.exp(s2 - m2) / z
        tau_ref[h] = jnp.broadcast_to(tau, tau_ref.shape[1:])


def _peer_route(q, sub_keys):
    t = q.shape[0]
    _, n_keys, half = sub_keys.shape
    tm = min(256, t)
    hps = 4
    return pl.pallas_call(
        functools.partial(_route_kernel, heads_per_step=hps),
        grid=(t // tm, PEER_HEADS // hps),
        in_specs=[pl.BlockSpec((tm, hps * 2 * half), lambda i, h: (i, h)),
                  pl.BlockSpec((2, n_keys, half), lambda i, h: (0, 0, 0))],
        out_specs=[pl.BlockSpec((hps, 2, n_keys, tm), lambda i, h: (h, 0, 0, i)),
                   pl.BlockSpec((hps, 2, n_keys, tm), lambda i, h: (h, 0, 0, i)),
                   pl.BlockSpec((hps, SUBLANE, tm), lambda i, h: (h, 0, i))],
        out_shape=[jax.ShapeDtypeStruct((PEER_HEADS, 2, n_keys, t), F32),
                   jax.ShapeDtypeStruct((PEER_HEADS, 2, n_keys, t), F32),
                   jax.ShapeDtypeStruct((PEER_HEADS, SUBLANE, t), F32)],
        compiler_params=_cparams(("arbitrary", "arbitrary"), 32),
        name="peer_route",
    )(q, sub_keys)


def _expert_kernel(x_ref, u_ref, v_ref, s1_ref, e1_ref, s2_ref, e2_ref, tau_ref, o_ref, acc_ref,
                   *, n_keys, rows_per_chunk):
    c = pl.program_id(1)
    tm = x_ref.shape[1]

    @pl.when(c == 0)
    def _():
        acc_ref[...] = jnp.zeros(acc_ref.shape, F32)

    act = jnp.dot(u_ref[...], x_ref[...], preferred_element_type=F32)
    pieces = []
    for il in range(rows_per_chunk):
        parts = []
        for lb in range(tm // LANE):
            ln = slice(lb * LANE, (lb + 1) * LANE)
            a = act[il * n_keys:(il + 1) * n_keys, ln]
            z = jnp.where(a[:1] != a[:1], 1.0, 0.0)
            g = jnp.zeros((n_keys, LANE), F32)
            for h in range(PEER_HEADS):
                ssum = s1_ref[h, 0, il:il + 1, ln] + s2_ref[h, 0, :, ln]
                gate = e1_ref[h, 0, il:il + 1, ln] * e2_ref[h, 0, :, ln]
                g = g + jnp.where(ssum >= tau_ref[h, 0:1, ln] + z, gate, 0.0)
            parts.append((g * _gelu_tanh(a)).astype(BF16))
        pieces.append(jnp.concatenate(parts, axis=1))
    w_t = jnp.concatenate(pieces, axis=0)
    acc_ref[...] += lax.dot_general(w_t, v_ref[...], (((0,), (0,)), ((), ())),
                                    preferred_element_type=F32)

    @pl.when(c == pl.num_programs(1) - 1)
    def _():
        o_ref[...] = acc_ref[...].astype(o_ref.dtype)


def _peer_experts(x_t, u_tab, v_tab, s, e, tau):
    d, t = x_t.shape
    n_exp = u_tab.shape[0]
    n_keys = s.shape[2]
    rows_per_chunk = SUBLANE
    ce = rows_per_chunk * n_keys
    tm = min(512, t)
    kern = functools.partial(_expert_kernel, n_keys=n_keys, rows_per_chunk=rows_per_chunk)
    return pl.pallas_call(
        kern,
        grid=(t // tm, n_exp // ce),
        in_specs=[pl.BlockSpec((d, tm), lambda i, c: (0, i), pipeline_mode=pl.Buffered(1)),
                  pl.BlockSpec((ce, d), lambda i, c: (c, 0)),
                  pl.BlockSpec((ce, d), lambda i, c: (c, 0)),
                  pl.BlockSpec((PEER_HEADS, 1, rows_per_chunk, tm), lambda i, c: (0, 0, c, i)),
                  pl.BlockSpec((PEER_HEADS, 1, rows_per_chunk, tm), lambda i, c: (0, 0, c, i)),
                  pl.BlockSpec((PEER_HEADS, 1, n_keys, tm), lambda i, c: (0, 1, 0, i)),
                  pl.BlockSpec((PEER_HEADS, 1, n_keys, tm), lambda i, c: (0, 1, 0, i)),
                  pl.BlockSpec((PEER_HEADS, SUBLANE, tm), lambda i, c: (0, 0, i))],
        out_specs=pl.BlockSpec((tm, d), lambda i, c: (i, 0)),
        out_shape=jax.ShapeDtypeStruct((t, d), BF16),
        scratch_shapes=[pltpu.VMEM((tm, d), F32)],
        compiler_params=_cparams(("arbitrary", "arbitrary"), 56),
        name="peer_experts",
    )(x_t, u_tab, v_tab, s, e, s, e, tau)


def _peer(h, h_t, w_q, sub_keys, u_tab, v_tab):
    q = _matmul(h, w_q, BF16)
    s, e, tau = _peer_route(q, sub_keys)
    return _peer_experts(h_t, u_tab, v_tab, s, e, tau)


def kernel(x, c, ctx, c_ctx, w_ada, b_ada, ln_g, ln_b, na_w_qkv, na_w_o, na_rpb, rg_w_in, rg_conv_w, rg_conv_b, rg_w_r, rg_b_r, rg_w_i, rg_b_i, rg_lam, rg_w_out, peer_w_q, peer_sub_keys, peer_u, peer_v):
    assert x.shape[0] == 1 and ctx.shape[0] == 1
    depth = w_ada.shape[0]
    d = x.shape[2]
    alpha = (2 * depth) ** 0.25
    n_mixers = 2
    lat, con = 0, 1

    cond8 = jnp.zeros((SUBLANE, d), F32).at[lat].set(c[0]).at[con].set(c_ctx)
    mod = _modulation(cond8, w_ada, b_ada)

    xl = x[0]
    xc = ctx[0]
    hl = _modulate(xl, mod, 0, lat)
    hc = _modulate(xc, mod, 0, con)

    for l in range(depth):
        ctx_out = l < depth - 1
        j = l // n_mixers
        ac = None
        if l % n_mixers == 0:
            w_qkv = _to_bf16(na_w_qkv, j)
            w_o = _to_bf16(na_w_o, j)
            qkv_l = _matmul(hl, w_qkv, BF16)
            qkv_c = _matmul(hc, w_qkv, BF16)
            bias_tab = _bias_table(na_rpb[j])
            al, w_proj = _na_attention(qkv_l, qkv_c, bias_tab), w_o
            if ctx_out:
                ac = _ctx_attention(qkv_c)
        else:
            w_in = _to_bf16(rg_w_in, j)
            w_out = _to_bf16(rg_w_out, j)
            nb = rg_w_r.shape[2]
            w_gate = jnp.concatenate([rg_w_r[j, 0], rg_w_i[j, 0], rg_w_r[j, 1], rg_w_i[j, 1]],
                                     axis=-1).astype(BF16)
            assert w_gate.shape[0] == nb
            rg_args = (rg_conv_w[j], rg_conv_b[j], w_gate, rg_b_r[j], rg_b_i[j], rg_lam[j])
            proj_c = _matmul(hc, w_in, F32)
            hs_fc, hs_bc, h_fin = _rg_scan(proj_c, *rg_args, jnp.zeros((SUBLANE, d), F32))
            proj_l = _matmul(hl, w_in, F32)
            hs_f, hs_b, _ = _rg_scan(proj_l, *rg_args, h_fin)
            al, w_proj = _rg_combine(hs_f, hs_b, proj_l), w_out
            if ctx_out:
                ac = _rg_combine(hs_fc, hs_bc, proj_c)

        w_q = _to_bf16(peer_w_q, l)
        u_tab = _to_bf16(peer_u, l)
        v_tab = _to_bf16(peer_v, l)
        nxt = (l + 1, 1, 0) if ctx_out else None

        xl, hl, hl_t = _proj_residual_ln(al, w_proj, xl, mod, l, 2, ln_g[l, 0], ln_b[l, 0], lat,
                                         alpha, (l, 4, 3))
        y_ffn = _peer(hl, hl_t, w_q, peer_sub_keys[l], u_tab, v_tab)
        xl, hl, _ = _residual_ln(xl, y_ffn, mod, l, 5, ln_g[l, 1], ln_b[l, 1], lat, alpha, nxt)
        if ctx_out:
            xc, hc, hc_t = _proj_residual_ln(ac, w_proj, xc, mod, l, 2, ln_g[l, 0], ln_b[l, 0],
                                             con, alpha, (l, 4, 3))
            y_cf = _peer(hc, hc_t, w_q, peer_sub_keys[l], u_tab, v_tab)
            xc, hc, _ = _residual_ln(xc, y_cf, mod, l, 5, ln_g[l, 1], ln_b[l, 1], con, alpha, nxt)
    return xl[None]
```

```python
import functools

import jax
import jax.numpy as jnp
from jax import lax
from jax.experimental import pallas as pl
from jax.experimental.pallas import tpu as pltpu

F32 = jnp.float32
BF16 = jnp.bfloat16

GRID_W = 64
WIN_R = 8
WIN_C = 16
HEAD_DIM = 128
RG_C = 8.0
CONV_W = 4
PEER_HEADS = 8
PEER_TOPK = 16
LN_EPS = 1e-6

V7X_VMEM_BYTES = 64 * 1024 * 1024
LANE = 128
SUBLANE = 8
MXU_DIM = 256

NEG = -1e30
GROUP_LANES = 2 * HEAD_DIM


def _cparams(sem, vmem_mb, flags=None):
    assert vmem_mb * 1024 * 1024 < V7X_VMEM_BYTES
    return pltpu.CompilerParams(dimension_semantics=sem, vmem_limit_bytes=vmem_mb * 1024 * 1024,
                                flags=flags)


def _tile(n, target, unit):
    t = min(target, n) // unit * unit
    while n % t:
        t -= unit
    return t


def _gelu_tanh(x):
    return 0.5 * x * (1.0 + jnp.tanh(0.7978845608028654 * (x + 0.044715 * (x * x * x))))


def _sigmoid(x):
    return 1.0 / (1.0 + jnp.exp(-x))


def _mod_kernel(c_ref, w_ref, b_ref, o_ref):
    c = c_ref[...]
    s = (c * _sigmoid(c)).astype(BF16)
    o_ref[0] = jnp.dot(s, w_ref[0].astype(BF16), preferred_element_type=F32) + b_ref[0]


def _modulation(cond8, w_ada, b_ada):
    depth, d, n = w_ada.shape
    tn = 512
    return pl.pallas_call(
        _mod_kernel,
        grid=(depth, n // tn),
        in_specs=[pl.BlockSpec((SUBLANE, d), lambda l, j: (0, 0)),
                  pl.BlockSpec((1, d, tn), lambda l, j: (l, 0, j)),
                  pl.BlockSpec((1, 1, tn), lambda l, j: (l, 0, j))],
        out_specs=pl.BlockSpec((1, SUBLANE, tn), lambda l, j: (l, 0, j)),
        out_shape=jax.ShapeDtypeStruct((depth, SUBLANE, n), F32),
        compiler_params=_cparams(("arbitrary", "arbitrary"), 40),
        name="modulation",
    )(cond8, w_ada, b_ada.reshape(depth, 1, n))


def _mod_spec(d, layer, chunk):
    return pl.BlockSpec((1, SUBLANE, d), lambda *_: (layer, 0, chunk))


def _modulate_kernel(x_ref, sc_ref, sh_ref, o_ref, *, row):
    sc = sc_ref[0, row:row + 1, :]
    sh = sh_ref[0, row:row + 1, :]
    o_ref[...] = (x_ref[...] * (1.0 + sc) + sh).astype(BF16)


def _modulate(x, mod, layer, row):
    t, d = x.shape
    tr = min(256, t)
    return pl.pallas_call(
        functools.partial(_modulate_kernel, row=row),
        grid=(t // tr,),
        in_specs=[pl.BlockSpec((tr, d), lambda i: (i, 0)),
                  _mod_spec(d, layer, 1), _mod_spec(d, layer, 0)],
        out_specs=pl.BlockSpec((tr, d), lambda i: (i, 0)),
        out_shape=jax.ShapeDtypeStruct((t, d), BF16),
        compiler_params=_cparams(("arbitrary",), 32),
        name="modulate",
    )(x, mod, mod)


def _ln_kernel(x_ref, y_ref, g_ref, lg_ref, lb_ref, *rest, row, alpha, with_h, with_ht):
    hto_ref = None
    if with_ht:
        sc_ref, sh_ref, xo_ref, ho_ref, hto_ref = rest
    elif with_h:
        sc_ref, sh_ref, xo_ref, ho_ref = rest
    else:
        (xo_ref,) = rest
    z = alpha * x_ref[...] + g_ref[0, row:row + 1, :] * y_ref[...].astype(F32)
    mu = jnp.mean(z, axis=-1, keepdims=True)
    zc = z - mu
    var = jnp.mean(zc * zc, axis=-1, keepdims=True)
    xn = zc * lax.rsqrt(var + LN_EPS) * lg_ref[...] + lb_ref[...]
    xo_ref[...] = xn
    if with_h:
        h = xn * (1.0 + sc_ref[0, row:row + 1, :]) + sh_ref[0, row:row + 1, :]
        ho_ref[...] = h.astype(BF16)
        if with_ht:
            hto_ref[...] = h.T.astype(BF16)


def _residual_ln(x, y, mod, layer, gate_chunk, ln_g, ln_b, row, alpha, nxt, with_ht=False):
    t, d = x.shape
    tr = min(256, t)
    with_h = nxt is not None
    in_specs = [pl.BlockSpec((tr, d), lambda i: (i, 0)),
                pl.BlockSpec((tr, d), lambda i: (i, 0)),
                _mod_spec(d, layer, gate_chunk),
                pl.BlockSpec((1, d), lambda i: (0, 0)),
                pl.BlockSpec((1, d), lambda i: (0, 0))]
    args = [x, y, mod, ln_g.reshape(1, d), ln_b.reshape(1, d)]
    out_specs = [pl.BlockSpec((tr, d), lambda i: (i, 0))]
    out_shape = [jax.ShapeDtypeStruct((t, d), F32)]
    if with_h:
        in_specs += [_mod_spec(d, nxt[0], nxt[1]), _mod_spec(d, nxt[0], nxt[2])]
        args += [mod, mod]
        out_specs.append(pl.BlockSpec((tr, d), lambda i: (i, 0)))
        out_shape.append(jax.ShapeDtypeStruct((t, d), BF16))
    if with_ht:
        out_specs.append(pl.BlockSpec((d, tr), lambda i: (0, i)))
        out_shape.append(jax.ShapeDtypeStruct((d, t), BF16))
    outs = pl.pallas_call(
        functools.partial(_ln_kernel, row=row, alpha=alpha, with_h=with_h, with_ht=with_ht),
        grid=(t // tr,),
        in_specs=in_specs, out_specs=out_specs, out_shape=out_shape,
        compiler_params=_cparams(("arbitrary",), 48),
        name="residual_ln",
    )(*args)
    outs = list(outs) + [None] * (3 - len(outs))
    return outs[0], outs[1], outs[2]


def _mm_kernel(a_ref, w_ref, o_ref):
    o_ref[...] = jnp.dot(a_ref[...], w_ref[...], preferred_element_type=F32).astype(o_ref.dtype)


def _matmul(a, w, out_dtype):
    m, k = a.shape
    _, n = w.shape
    tm = _tile(m, 1024, SUBLANE)
    tn = _tile(n, 1024, LANE)
    return pl.pallas_call(
        _mm_kernel,
        grid=(m // tm, n // tn),
        in_specs=[pl.BlockSpec((tm, k), lambda i, j: (i, 0)),
                  pl.BlockSpec((k, tn), lambda i, j: (0, j))],
        out_specs=pl.BlockSpec((tm, tn), lambda i, j: (i, j)),
        out_shape=jax.ShapeDtypeStruct((m, n), out_dtype),
        compiler_params=_cparams(("arbitrary", "arbitrary"), 48),
        name="matmul",
    )(a, w)


def _proj_ln_kernel(a_ref, w_ref, x_ref, g_ref, lg_ref, lb_ref, sc_ref, sh_ref,
                    xo_ref, ho_ref, hto_ref, *, row, alpha):
    y = jnp.dot(a_ref[...], w_ref[...], preferred_element_type=F32)
    z = alpha * x_ref[...] + g_ref[0, row:row + 1, :] * y
    mu = jnp.mean(z, axis=-1, keepdims=True)
    zc = z - mu
    var = jnp.mean(zc * zc, axis=-1, keepdims=True)
    xn = zc * lax.rsqrt(var + LN_EPS) * lg_ref[...] + lb_ref[...]
    xo_ref[...] = xn
    h = xn * (1.0 + sc_ref[0, row:row + 1, :]) + sh_ref[0, row:row + 1, :]
    ho_ref[...] = h.astype(BF16)
    hto_ref[...] = h.T.astype(BF16)


def _proj_residual_ln(a, w, x, mod, layer, gate_chunk, ln_g, ln_b, row, alpha, nxt):
    t, k = a.shape
    d = w.shape[1]
    tr = min(128, t)
    rows = lambda i: (i, 0)
    fixed = lambda i: (0, 0)
    return pl.pallas_call(
        functools.partial(_proj_ln_kernel, row=row, alpha=alpha),
        grid=(t // tr,),
        in_specs=[pl.BlockSpec((tr, k), rows),
                  pl.BlockSpec((k, d), fixed, pipeline_mode=pl.Buffered(1)),
                  pl.BlockSpec((tr, d), rows),
                  _mod_spec(d, layer, gate_chunk),
                  pl.BlockSpec((1, d), fixed), pl.BlockSpec((1, d), fixed),
                  _mod_spec(d, nxt[0], nxt[1]), _mod_spec(d, nxt[0], nxt[2])],
        out_specs=[pl.BlockSpec((tr, d), rows), pl.BlockSpec((tr, d), rows),
                   pl.BlockSpec((d, tr), lambda i: (0, i))],
        out_shape=[jax.ShapeDtypeStruct((t, d), F32), jax.ShapeDtypeStruct((t, d), BF16),
                   jax.ShapeDtypeStruct((d, t), BF16)],
        compiler_params=_cparams(("arbitrary",), 56),
        name="proj_residual_ln",
    )(a, w, x, mod, ln_g.reshape(1, d), ln_b.reshape(1, d), mod, mod)


def _cast_kernel(w_ref, o_ref):
    o_ref[...] = w_ref[0].astype(BF16)


def _to_bf16(w_stack, idx):
    _, r, c = w_stack.shape
    block_bytes = 8 * 1024 * 1024
    tr = _tile(r, max(16, block_bytes // (4 * c)), 16)
    return pl.pallas_call(
        _cast_kernel,
        grid=(r // tr,),
        in_specs=[pl.BlockSpec((1, tr, c), lambda i: (idx, i, 0))],
        out_specs=pl.BlockSpec((tr, c), lambda i: (i, 0)),
        out_shape=jax.ShapeDtypeStruct((r, c), BF16),
        compiler_params=_cparams(("arbitrary",), 32),
        name="to_bf16",
    )(w_stack)


def _bias_kernel(rpb_ref, o_ref):
    h = pl.program_id(0)
    n_dr = 2 * WIN_R - 1
    n_dc = 2 * WIN_C - 1
    w = lax.broadcasted_iota(jnp.int32, (GRID_W, GRID_W), 0)
    col = lax.broadcasted_iota(jnp.int32, (GRID_W, GRID_W), 1)
    dcm = col - w + (WIN_C - 1)
    cs = jnp.clip(w - WIN_C // 2, 0, GRID_W - WIN_C)
    valid = (col >= cs) & (col < cs + WIN_C)
    planes = []
    for dr in range(n_dr):
        acc = jnp.full((GRID_W, GRID_W), NEG, F32)
        for dc in range(n_dc):
            acc = jnp.where(dcm == dc, rpb_ref[(h * n_dr + dr) * n_dc + dc], acc)
        planes.append(jnp.where(valid, acc, NEG))
    for off in range(WIN_R):
        o_ref[0, off] = jnp.concatenate([planes[off + a] for a in range(WIN_R)], axis=1)


def _bias_table(rpb):
    heads = rpb.shape[0]
    return pl.pallas_call(
        _bias_kernel,
        grid=(heads,),
        in_specs=[pl.BlockSpec(memory_space=pltpu.SMEM)],
        out_specs=pl.BlockSpec((1, WIN_R, GRID_W, WIN_R * GRID_W), lambda h: (h, 0, 0, 0)),
        out_shape=jax.ShapeDtypeStruct((heads, WIN_R, GRID_W, WIN_R * GRID_W), F32),
        compiler_params=_cparams(("arbitrary",), 32),
        name="na_bias_table",
    )(rpb.reshape(-1))


def _na_kernel(q_ref, k_ref, v_ref, kc_ref, vc_ref, tb_ref, o_ref, *, rows_per_step, n_rows, scale):
    blk = pl.program_id(1)
    win = WIN_R * GRID_W
    nt = (((1,), (1,)), ((), ()))
    heads = range(GROUP_LANES // HEAD_DIM)
    lanes = [slice(h * HEAD_DIM, (h + 1) * HEAD_DIM) for h in heads]
    rows, kstarts, offs = [], [], []
    for rl in range(rows_per_step):
        r = blk * rows_per_step + rl
        rs = jnp.clip(r - WIN_R // 2, 0, n_rows - WIN_R)
        offs.append(rs - r + (WIN_R - 1))
        kstarts.append(pl.multiple_of(rs * GRID_W, GRID_W))
        rows.append(slice(rl * GRID_W, (rl + 1) * GRID_W))
    units = [(h, i) for h in heads for i in range(rows_per_step)]

    q_all = [q_ref[:, lanes[h]] for h in heads]
    s_ctx = [lax.dot_general(q_all[h], kc_ref[:, lanes[h]], nt, preferred_element_type=F32) * scale
             for h in heads]
    m_ctx = [jnp.max(s_ctx[h], axis=-1, keepdims=True) for h in heads]
    s_loc, m, p, l_loc, o_loc = {}, {}, {}, {}, {}

    def stage(k, u):
        h, i = u
        if k == 0:
            s_loc[u] = lax.dot_general(q_all[h][rows[i]], k_ref[pl.ds(kstarts[i], win), lanes[h]],
                                       nt, preferred_element_type=F32) * scale + tb_ref[h, offs[i]]
        elif k == 1:
            m[u] = jnp.maximum(jnp.max(s_loc[u], axis=-1, keepdims=True), m_ctx[h][rows[i]])
        elif k == 2:
            p[u] = jnp.exp(s_loc[u] - m[u])
        elif k == 3:
            l_loc[u] = jnp.sum(p[u], axis=-1, keepdims=True)
        else:
            o_loc[u] = jnp.dot(p[u].astype(BF16), v_ref[pl.ds(kstarts[i], win), lanes[h]],
                               preferred_element_type=F32)

    n_stages = 5
    for tick in range(len(units) + n_stages - 1):
        for k in range(n_stages):
            j = tick - k
            if 0 <= j < len(units):
                stage(k, units[j])

    for h in heads:
        col = lambda d: jnp.concatenate([d[h, i] for i in range(rows_per_step)], axis=0)
        p_ctx = jnp.exp(s_ctx[h] - col(m))
        denom = col(l_loc) + jnp.sum(p_ctx, axis=-1, keepdims=True)
        o = col(o_loc) + jnp.dot(p_ctx.astype(BF16), vc_ref[:, lanes[h]],
                                 preferred_element_type=F32)
        o_ref[:, lanes[h]] = (o / denom).astype(BF16)


def _na_attention(qkv, qkv_ctx, bias_tab):
    t, d3 = qkv.shape
    d = d3 // 3
    n_ctx = qkv_ctx.shape[0]
    n_rows = t // GRID_W
    rows_per_step = _tile(n_rows, 32, 1)
    groups = d // GROUP_LANES
    tq = rows_per_step * GRID_W
    hg = GROUP_LANES // HEAD_DIM
    kern = functools.partial(_na_kernel, rows_per_step=rows_per_step, n_rows=n_rows,
                             scale=HEAD_DIM ** -0.5)
    return pl.pallas_call(
        kern,
        grid=(groups, n_rows // rows_per_step),
        in_specs=[pl.BlockSpec((tq, GROUP_LANES), lambda g, b: (b, g)),
                  pl.BlockSpec((t, GROUP_LANES), lambda g, b: (0, groups + g)),
                  pl.BlockSpec((t, GROUP_LANES), lambda g, b: (0, 2 * groups + g)),
                  pl.BlockSpec((n_ctx, GROUP_LANES), lambda g, b: (0, groups + g)),
                  pl.BlockSpec((n_ctx, GROUP_LANES), lambda g, b: (0, 2 * groups + g)),
                  pl.BlockSpec((hg, WIN_R, GRID_W, WIN_R * GRID_W), lambda g, b: (g, 0, 0, 0))],
        out_specs=pl.BlockSpec((tq, GROUP_LANES), lambda g, b: (b, g)),
        out_shape=jax.ShapeDtypeStruct((t, d), BF16),
        compiler_params=_cparams(("arbitrary", "arbitrary"), 60),
        name="na_attention",
    )(qkv, qkv, qkv, qkv_ctx, qkv_ctx, bias_tab)


def _ctx_attn_kernel(q_ref, k_ref, v_ref, o_ref, *, scale):
    nt = (((1,), (1,)), ((), ()))
    for hl in range(GROUP_LANES // HEAD_DIM):
        lanes = slice(hl * HEAD_DIM, (hl + 1) * HEAD_DIM)
        s = lax.dot_general(q_ref[:, lanes], k_ref[:, lanes], nt, preferred_element_type=F32) * scale
        m = jnp.max(s, axis=-1, keepdims=True)
        p = jnp.exp(s - m)
        denom = jnp.sum(p, axis=-1, keepdims=True)
        o = jnp.dot(p.astype(BF16), v_ref[:, lanes], preferred_element_type=F32)
        o_ref[:, lanes] = (o / denom).astype(BF16)


def _ctx_attention(qkv_ctx):
    n_ctx, d3 = qkv_ctx.shape
    d = d3 // 3
    groups = d // GROUP_LANES
    return pl.pallas_call(
        functools.partial(_ctx_attn_kernel, scale=HEAD_DIM ** -0.5),
        grid=(groups,),
        in_specs=[pl.BlockSpec((n_ctx, GROUP_LANES), lambda g: (0, g)),
                  pl.BlockSpec((n_ctx, GROUP_LANES), lambda g: (0, groups + g)),
                  pl.BlockSpec((n_ctx, GROUP_LANES), lambda g: (0, 2 * groups + g))],
        out_specs=pl.BlockSpec((n_ctx, GROUP_LANES), lambda g: (0, g)),
        out_shape=jax.ShapeDtypeStruct((n_ctx, d), BF16),
        compiler_params=_cparams(("arbitrary",), 32),
        name="ctx_attention",
    )(qkv_ctx, qkv_ctx, qkv_ctx)


def _scan_block(a, u, h_in, reverse):
    tb, c = a.shape
    ng = tb // SUBLANE
    a3 = a.reshape(ng, SUBLANE, c)
    u3 = u.reshape(ng, SUBLANE, c)
    j = lax.broadcasted_iota(jnp.int32, (ng, SUBLANE, c), 1)
    d = 1
    while d < SUBLANE:
        if reverse:
            keep = j < SUBLANE - d
            shift = SUBLANE - d
        else:
            keep = j >= d
            shift = d
        a_s = jnp.where(keep, pltpu.roll(a3, shift, 1), 1.0)
        u_s = jnp.where(keep, pltpu.roll(u3, shift, 1), 0.0)
        u3 = u3 + a3 * u_s
        a3 = a3 * a_s
        d *= 2
    hs = [None] * ng
    h = h_in
    order = range(ng - 1, -1, -1) if reverse else range(ng)
    edge = 0 if reverse else SUBLANE - 1
    for g in order:
        hg = u3[g] + a3[g] * h
        hs[g] = hg
        h = hg[edge:edge + 1, :]
    return jnp.concatenate(hs, axis=0), h


def _rg_kernel(curf_ref, prevf_ref, nextf_ref, curb_ref, prevb_ref, nextb_ref,
               cw_ref, cb_ref, wg_ref, br_ref, bi_ref, lam_ref, h0_ref,
               hsf_ref, hsb_ref, hfin_ref, xbuf_ref, carry_ref, *, tb, n_tb):
    i = pl.program_id(1)

    @pl.when(i == 0)
    def _():
        carry_ref[...] = h0_ref[...]

    cw = cw_ref[...]
    cb = cb_ref[...]
    halo = SUBLANE

    def gates(cur_ref, prev_ref, next_ref, blk, e):
        prev = jnp.where(blk == 0, 0.0, prev_ref[...])
        nxt = jnp.where(blk == n_tb - 1, 0.0, next_ref[...])
        xbuf_ref[e, 0:halo, :] = prev
        xbuf_ref[e, halo:halo + tb, :] = cur_ref[...]
        xbuf_ref[e, halo + tb:2 * halo + tb, :] = nxt
        left = CONV_W // 2
        xc = cb
        for k in range(CONV_W):
            xc = xc + xbuf_ref[e, halo - left + k:halo - left + k + tb, :] * cw[k:k + 1, :]
        c = xc.shape[1]
        z = jnp.dot(xc.astype(BF16), wg_ref[0, :, 2 * c * e:2 * c * (e + 1)],
                    preferred_element_type=F32)
        r = _sigmoid(z[:, :c] + br_ref[e:e + 1, :])
        ig = _sigmoid(z[:, c:] + bi_ref[e:e + 1, :])
        lam = lam_ref[e:e + 1, :]
        softplus = jnp.maximum(-lam, 0.0) + jnp.log(1.0 + jnp.exp(-jnp.abs(lam)))
        log_a = (-RG_C) * r * softplus
        a = jnp.exp(log_a)
        u = jnp.sqrt(1.0 - jnp.exp(2.0 * log_a)) * ig * xc
        return a, u

    a_f, u_f = gates(curf_ref, prevf_ref, nextf_ref, i, 0)
    h_f, last_f = _scan_block(a_f, u_f, carry_ref[0:1, :], reverse=False)
    hsf_ref[...] = h_f
    carry_ref[0:1, :] = last_f

    a_b, u_b = gates(curb_ref, prevb_ref, nextb_ref, n_tb - 1 - i, 1)
    h_b, last_b = _scan_block(a_b, u_b, carry_ref[1:2, :], reverse=True)
    hsb_ref[...] = h_b
    carry_ref[1:2, :] = last_b

    @pl.when(i == n_tb - 1)
    def _():
        hfin_ref[...] = carry_ref[...]


def _rg_scan(proj, conv_w, conv_b, w_gate, b_r, b_i, lam, h0):
    t, d2 = proj.shape
    d = d2 // 2
    c = MXU_DIM
    nb = d // c
    tb = min(512, t)
    n_tb = t // tb
    hb = tb // SUBLANE
    n_halo = t // SUBLANE

    def cur_f(n, i): return (i, nb + n)
    def prev_f(n, i): return (jnp.maximum(i * hb - 1, 0), nb + n)
    def next_f(n, i): return (jnp.minimum((i + 1) * hb, n_halo - 1), nb + n)
    def cur_b(n, i): return (n_tb - 1 - i, nb + n)
    def prev_b(n, i): return (jnp.maximum((n_tb - 1 - i) * hb - 1, 0), nb + n)
    def next_b(n, i): return (jnp.minimum((n_tb - i) * hb, n_halo - 1), nb + n)

    row = lambda n, i: (0, n)
    kern = functools.partial(_rg_kernel, tb=tb, n_tb=n_tb)
    return pl.pallas_call(
        kern,
        grid=(nb, n_tb),
        in_specs=[pl.BlockSpec((tb, c), cur_f), pl.BlockSpec((SUBLANE, c), prev_f),
                  pl.BlockSpec((SUBLANE, c), next_f),
                  pl.BlockSpec((tb, c), cur_b), pl.BlockSpec((SUBLANE, c), prev_b),
                  pl.BlockSpec((SUBLANE, c), next_b),
                  pl.BlockSpec((CONV_W, c), row), pl.BlockSpec((1, c), row),
                  pl.BlockSpec((1, c, 4 * c), lambda n, i: (n, 0, 0)),
                  pl.BlockSpec((2, c), row), pl.BlockSpec((2, c), row), pl.BlockSpec((2, c), row),
                  pl.BlockSpec((SUBLANE, c), row)],
        out_specs=[pl.BlockSpec((tb, c), lambda n, i: (i, n)),
                   pl.BlockSpec((tb, c), lambda n, i: (n_tb - 1 - i, n)),
                   pl.BlockSpec((SUBLANE, c), row)],
        out_shape=[jax.ShapeDtypeStruct((t, d), F32), jax.ShapeDtypeStruct((t, d), F32),
                   jax.ShapeDtypeStruct((SUBLANE, d), F32)],
        scratch_shapes=[pltpu.VMEM((2, tb + 2 * SUBLANE, c), F32), pltpu.VMEM((SUBLANE, c), F32)],
        compiler_params=_cparams(("arbitrary", "arbitrary"), 32),
        name="rg_scan",
    )(proj, proj, proj, proj, proj, proj, conv_w, conv_b.reshape(1, d), w_gate, b_r, b_i, lam, h0)


def _rg_combine_kernel(hf_ref, hb_ref, g_ref, o_ref):
    o_ref[...] = ((hf_ref[...] + hb_ref[...]) * _gelu_tanh(g_ref[...])).astype(BF16)


def _rg_combine(hs_f, hs_b, proj):
    t, d = hs_f.shape
    tr = min(256, t)
    spec = pl.BlockSpec((tr, d), lambda i: (i, 0))
    return pl.pallas_call(
        _rg_combine_kernel,
        grid=(t // tr,),
        in_specs=[spec, spec, spec],
        out_specs=spec,
        out_shape=jax.ShapeDtypeStruct((t, d), BF16),
        compiler_params=_cparams(("arbitrary",), 40),
        name="rg_combine",
    )(hs_f, hs_b, proj)


def _top_values(s, k):
    out = []
    for _ in range(k):
        m = jnp.max(s, axis=0, keepdims=True)
        s = jnp.where(s == m, -jnp.inf, s)
        out.append(m)
    return out


def _route_head(s1, s2):
    k = PEER_TOPK
    top1 = _top_values(s1, k)
    top2 = _top_values(s2, k)
    b16 = jnp.concatenate(top2, axis=0)
    b8 = b16[:SUBLANE]
    rank = lax.broadcasted_iota(jnp.int32, b8.shape, 0)
    pieces = [top1[0] + b16, top1[1] + b8]
    for a in range(2, SUBLANE):
        pieces.append(jnp.where(rank < k // (a + 1), top1[a] + b8, -jnp.inf))
    pieces.append(jnp.concatenate(top1[SUBLANE:], axis=0) + top2[0])
    cand = jnp.concatenate(pieces, axis=0)
    tau = _top_values(cand, k)[-1]
    m1, m2 = top1[0], top2[0]
    z = jnp.sum(jnp.where(cand >= tau, jnp.exp(cand - (m1 + m2)), 0.0), axis=0, keepdims=True)
    return tau, m1, m2, z


def _route_kernel(q_ref, keys_ref, s_ref, e_ref, tau_ref, *, heads_per_step):
    nt = (((1,), (1,)), ((), ()))
    half = keys_ref.shape[2]
    k1 = keys_ref[0].astype(BF16)
    k2 = keys_ref[1].astype(BF16)
    for h in range(heads_per_step):
        base = 2 * half * h
        s1 = lax.dot_general(k1, q_ref[:, base:base + half], nt, preferred_element_type=F32)
        s2 = lax.dot_general(k2, q_ref[:, base + half:base + 2 * half], nt,
                             preferred_element_type=F32)
        tau, m1, m2, z = _route_head(s1, s2)
        s_ref[h, 0] = s1
        s_ref[h, 1] = s2
        e_ref[h, 0] = jnp.exp(s1 - m1)
        e_ref[h, 1] = jnp.exp(s2 - m2) / z
        tau_ref[h] = jnp.broadcast_to(tau, tau_ref.shape[1:])


def _peer_route(q, sub_keys):
    t = q.shape[0]
    _, n_keys, half = sub_keys.shape
    tm = min(256, t)
    hps = 4
    return pl.pallas_call(
        functools.partial(_route_kernel, heads_per_step=hps),
        grid=(t // tm, PEER_HEADS // hps),
        in_specs=[pl.BlockSpec((tm, hps * 2 * half), lambda i, h: (i, h)),
                  pl.BlockSpec((2, n_keys, half), lambda i, h: (0, 0, 0))],
        out_specs=[pl.BlockSpec((hps, 2, n_keys, tm), lambda i, h: (h, 0, 0, i)),
                   pl.BlockSpec((hps, 2, n_keys, tm), lambda i, h: (h, 0, 0, i)),
                   pl.BlockSpec((hps, SUBLANE, tm), lambda i, h: (h, 0, i))],
        out_shape=[jax.ShapeDtypeStruct((PEER_HEADS, 2, n_keys, t), F32),
                   jax.ShapeDtypeStruct((PEER_HEADS, 2, n_keys, t), F32),
                   jax.ShapeDtypeStruct((PEER_HEADS, SUBLANE, t), F32)],
        compiler_params=_cparams(("arbitrary", "arbitrary"), 32),
        name="peer_route",
    )(q, sub_keys)


def _expert_kernel(x_ref, u_ref, v_ref, s1_ref, e1_ref, s2_ref, e2_ref, tau_ref, o_ref, acc_ref,
                   *, n_keys, rows_per_chunk):
    c = pl.program_id(1)
    tm = x_ref.shape[1]

    @pl.when(c == 0)
    def _():
        acc_ref[...] = jnp.zeros(acc_ref.shape, F32)

    act = jnp.dot(u_ref[...], x_ref[...], preferred_element_type=F32)
    pieces = []
    for il in range(rows_per_chunk):
        parts = []
        for lb in range(tm // LANE):
            ln = slice(lb * LANE, (lb + 1) * LANE)
            a = act[il * n_keys:(il + 1) * n_keys, ln]
            z = jnp.where(a[:1] != a[:1], 1.0, 0.0)
            g = jnp.zeros((n_keys, LANE), F32)
            for h in range(PEER_HEADS):
                ssum = s1_ref[h, 0, il:il + 1, ln] + s2_ref[h, 0, :, ln]
                gate = e1_ref[h, 0, il:il + 1, ln] * e2_ref[h, 0, :, ln]
                g = g + jnp.where(ssum >= tau_ref[h, 0:1, ln] + z, gate, 0.0)
            parts.append((g * _gelu_tanh(a)).astype(BF16))
        pieces.append(jnp.concatenate(parts, axis=1))
    w_t = jnp.concatenate(pieces, axis=0)
    acc_ref[...] += lax.dot_general(w_t, v_ref[...], (((0,), (0,)), ((), ())),
                                    preferred_element_type=F32)

    @pl.when(c == pl.num_programs(1) - 1)
    def _():
        o_ref[...] = acc_ref[...].astype(o_ref.dtype)


def _peer_experts(x_t, u_tab, v_tab, s, e, tau):
    d, t = x_t.shape
    n_exp = u_tab.shape[0]
    n_keys = s.shape[2]
    rows_per_chunk = SUBLANE
    ce = rows_per_chunk * n_keys
    tm = min(512, t)
    kern = functools.partial(_expert_kernel, n_keys=n_keys, rows_per_chunk=rows_per_chunk)
    return pl.pallas_call(
        kern,
        grid=(t // tm, n_exp // ce),
        in_specs=[pl.BlockSpec((d, tm), lambda i, c: (0, i), pipeline_mode=pl.Buffered(1)),
                  pl.BlockSpec((ce, d), lambda i, c: (c, 0)),
                  pl.BlockSpec((ce, d), lambda i, c: (c, 0)),
                  pl.BlockSpec((PEER_HEADS, 1, rows_per_chunk, tm), lambda i, c: (0, 0, c, i)),
                  pl.BlockSpec((PEER_HEADS, 1, rows_per_chunk, tm), lambda i, c: (0, 0, c, i)),
                  pl.BlockSpec((PEER_HEADS, 1, n_keys, tm), lambda i, c: (0, 1, 0, i)),
                  pl.BlockSpec((PEER_HEADS, 1, n_keys, tm), lambda i, c: (0, 1, 0, i)),
                  pl.BlockSpec((PEER_HEADS, SUBLANE, tm), lambda i, c: (0, 0, i))],
        out_specs=pl.BlockSpec((tm, d), lambda i, c: (i, 0)),
        out_shape=jax.ShapeDtypeStruct((t, d), BF16),
        scratch_shapes=[pltpu.VMEM((tm, d), F32)],
        compiler_params=_cparams(("arbitrary", "arbitrary"), 56),
        name="peer_experts",
    )(x_t, u_tab, v_tab, s, e, s, e, tau)


def _peer(h, h_t, w_q, sub_keys, u_tab, v_tab):
    q = _matmul(h, w_q, BF16)
    s, e, tau = _peer_route(q, sub_keys)
    return _peer_experts(h_t, u_tab, v_tab, s, e, tau)


def kernel(x, c, ctx, c_ctx, w_ada, b_ada, ln_g, ln_b, na_w_qkv, na_w_o, na_rpb, rg_w_in, rg_conv_w, rg_conv_b, rg_w_r, rg_b_r, rg_w_i, rg_b_i, rg_lam, rg_w_out, peer_w_q, peer_sub_keys, peer_u, peer_v):
    assert x.shape[0] == 1 and ctx.shape[0] == 1
    depth = w_ada.shape[0]
    d = x.shape[2]
    alpha = (2 * depth) ** 0.25
    n_mixers = 2
    lat, con = 0, 1

    cond8 = jnp.zeros((SUBLANE, d), F32).at[lat].set(c[0]).at[con].set(c_ctx)
    mod = _modulation(cond8, w_ada, b_ada)

    xl = x[0]
    xc = ctx[0]
    hl = _modulate(xl, mod, 0, lat)
    hc = _modulate(xc, mod, 0, con)

    for l in range(depth):
        ctx_out = l < depth - 1
        j = l // n_mixers
        ac = None
        if l % n_mixers == 0:
            w_qkv = _to_bf16(na_w_qkv, j)
            w_o = _to_bf16(na_w_o, j)
            qkv_l = _matmul(hl, w_qkv, BF16)
            qkv_c = _matmul(hc, w_qkv, BF16)
            bias_tab = _bias_table(na_rpb[j])
            al, w_proj = _na_attention(qkv_l, qkv_c, bias_tab), w_o
            if ctx_out:
                ac = _ctx_attention(qkv_c)
        else:
            w_in = _to_bf16(rg_w_in, j)
            w_out = _to_bf16(rg_w_out, j)
            nb = rg_w_r.shape[2]
            w_gate = jnp.concatenate([rg_w_r[j, 0], rg_w_i[j, 0], rg_w_r[j, 1], rg_w_i[j, 1]],
                                     axis=-1).astype(BF16)
            assert w_gate.shape[0] == nb
            rg_args = (rg_conv_w[j], rg_conv_b[j], w_gate, rg_b_r[j], rg_b_i[j], rg_lam[j])
            proj_c = _matmul(hc, w_in, F32)
            hs_fc, hs_bc, h_fin = _rg_scan(proj_c, *rg_args, jnp.zeros((SUBLANE, d), F32))
            proj_l = _matmul(hl, w_in, F32)
            hs_f, hs_b, _ = _rg_scan(proj_l, *rg_args, h_fin)
            al, w_proj = _rg_combine(hs_f, hs_b, proj_l), w_out
            if ctx_out:
                ac = _rg_combine(hs_fc, hs_bc, proj_c)

        w_q = _to_bf16(peer_w_q, l)
        u_tab = _to_bf16(peer_u, l)
        v_tab = _to_bf16(peer_v, l)
        nxt = (l + 1, 1, 0) if ctx_out else None

        xl, hl, hl_t = _proj_residual_ln(al, w_proj, xl, mod, l, 2, ln_g[l, 0], ln_b[l, 0], lat,
                                         alpha, (l, 4, 3))
        y_ffn = _peer(hl, hl_t, w_q, peer_sub_keys[l], u_tab, v_tab)
        xl, hl, _ = _residual_ln(xl, y_ffn, mod, l, 5, ln_g[l, 1], ln_b[l, 1], lat, alpha, nxt)
        if ctx_out:
            xc, hc, hc_t = _proj_residual_ln(ac, w_proj, xc, mod, l, 2, ln_g[l, 0], ln_b[l, 0],
                                             con, alpha, (l, 4, 3))
            y_cf = _peer(hc, hc_t, w_q, peer_sub_keys[l], u_tab, v_tab)
            xc, hc, _ = _residual_ln(xc, y_cf, mod, l, 5, ln_g[l, 1], ln_b[l, 1], con, alpha, nxt)
    return xl[None]
```
